```python
import jax, jax.numpy as jnp
from jax import lax
import numpy as np

D_MODEL = 1024
BATCH = 8
SEQ = 4096
DEPTH = 2

N_A_LAYERS = (DEPTH + 1) // 2
N_B_LAYERS = DEPTH - N_A_LAYERS
HGRN_DK = 128
HGRN_DV = 128
HGRN_HEADS = D_MODEL // HGRN_DK
HGRN_WIDTH = HGRN_HEADS * HGRN_DK
HGRN_CHUNK = 64
FOX_HEAD_DIM = 64
FOX_HEADS = D_MODEL // FOX_HEAD_DIM
FOX_WIDTH = FOX_HEADS * FOX_HEAD_DIM
FOX_QBLOCK = 128
FOX_GATE_BIAS_OFFSET = 3.0
D_FF = ((8 * D_MODEL // 3 + 255) // 256) * 256
PLE_DIM = 256
NORM_EPS = 1e-6

kernel_name = "yoco_hgrn2_fox_macaron_hybrid"


def rmsnorm(x, g):
    x32 = x.astype(jnp.float32)
    y = x32 * lax.rsqrt(jnp.mean(jnp.square(x32), axis=-1, keepdims=True) + NORM_EPS)
    return (y * g.astype(jnp.float32)).astype(x.dtype)


def swiglu(x, w_in, w_out):
    gate, up = jnp.split(x @ w_in, 2, axis=-1)
    return (jax.nn.silu(gate) * up) @ w_out


def hgrn2_chunked_scan(q, log_f, k, v):
    b_, s_, h_, dk = q.shape
    dv = v.shape[-1]
    n_chunks = s_ // HGRN_CHUNK

    def to_chunks(t):
        return t.astype(jnp.float32).reshape(b_, n_chunks, HGRN_CHUNK, h_, t.shape[-1]).transpose(1, 0, 3, 2, 4)

    qc, gc, kc, vc = to_chunks(q), to_chunks(log_f), to_chunks(k), to_chunks(v)
    causal = jnp.asarray(np.tril(np.ones((HGRN_CHUNK, HGRN_CHUNK), dtype=bool)))

    def step(state, inp):
        qb, gb, kb, vb = inp
        cum = jnp.cumsum(gb, axis=2)
        o_inter = jnp.einsum('bhtk,bhkv->bhtv', qb * jnp.exp(cum), state)
        rel = cum[:, :, :, None, :] - cum[:, :, None, :, :]
        decay = jnp.exp(jnp.where(causal[:, :, None], rel, -jnp.inf))
        scores = jnp.einsum('bhtk,bhtsk,bhsk->bhts', qb, decay, kb)
        o_intra = jnp.einsum('bhts,bhsv->bhtv', scores, vb)
        last = cum[:, :, -1, :]
        new_state = jnp.exp(last)[..., None] * state + jnp.einsum(
            'bhsk,bhsv->bhkv', kb * jnp.exp(last[:, :, None, :] - cum), vb)
        return new_state, o_inter + o_intra

    state0 = jnp.zeros((b_, h_, dk, dv), jnp.float32)
    _, out = lax.scan(step, state0, (qc, gc, kc, vc))
    return out.transpose(1, 0, 3, 2, 4).reshape(b_, s_, h_, dv)


def hgrn2_mixer(xn, w_in, lb, out_norm, w_out):
    b_, s_, _ = xn.shape
    q, fz, inp, g = jnp.split(xn @ w_in, 4, axis=-1)
    fz32 = fz.astype(jnp.float32)
    log_f = jnp.logaddexp(jnp.log(lb), jnp.log1p(-lb) + jax.nn.log_sigmoid(fz32))
    k = (1.0 - lb) * jax.nn.sigmoid(-fz32)
    heads = lambda t: t.reshape(b_, s_, HGRN_HEADS, -1)
    o = hgrn2_chunked_scan(heads(q), heads(log_f), heads(k), heads(inp))
    o = rmsnorm(o, out_norm).astype(xn.dtype).reshape(b_, s_, HGRN_WIDTH)
    return (o * jax.nn.silu(g)) @ w_out


def shared_kv(h, kv_norm, w_kvf, b_f):
    b_, s_, _ = h.shape
    kvf = rmsnorm(h, kv_norm) @ w_kvf
    k = kvf[..., :FOX_WIDTH].reshape(b_, s_, FOX_HEADS, FOX_HEAD_DIM)
    v = kvf[..., FOX_WIDTH:2 * FOX_WIDTH].reshape(b_, s_, FOX_HEADS, FOX_HEAD_DIM)
    log_f = jax.nn.log_sigmoid((kvf[..., 2 * FOX_WIDTH:] + b_f).astype(jnp.float32))
    c = jnp.cumsum(log_f, axis=1).transpose(0, 2, 1)
    return k, v, c


def forgetting_attention(q, k, v, c):
    s_ = q.shape[1]
    scale = FOX_HEAD_DIM ** -0.5
    outs = []
    for blk in range(s_ // FOX_QBLOCK):
        t0 = blk * FOX_QBLOCK
        t1 = t0 + FOX_QBLOCK
        logits = jnp.einsum('bthd,bshd->bhts', q[:, t0:t1], k[:, :t1]).astype(jnp.float32) * scale
        logits = logits + c[:, :, t0:t1, None] - c[:, :, None, :t1]
        mask = jnp.asarray((t0 + np.arange(FOX_QBLOCK))[:, None] >= np.arange(t1)[None, :])
        logits = jnp.where(mask, logits, -jnp.inf)
        probs = jax.nn.softmax(logits, axis=-1).astype(v.dtype)
        outs.append(jnp.einsum('bhts,bshd->bthd', probs, v[:, :t1]))
    return jnp.concatenate(outs, axis=1)


def fox_mixer(xn, w_qg, w_out, k, v, c):
    b_, s_, _ = xn.shape
    q, g = jnp.split(xn @ w_qg, 2, axis=-1)
    q = q.reshape(b_, s_, FOX_HEADS, FOX_HEAD_DIM)
    o = forgetting_attention(q, k, v, c).reshape(b_, s_, FOX_WIDTH)
    return (o * jax.nn.sigmoid(g)) @ w_out


def setup_inputs(seed: int = 0) -> dict:
    key = jax.random.key(seed)
    ks = iter(jax.random.split(key, 32))
    nrm = lambda shape, scale: jax.random.normal(next(ks), shape, jnp.float32) * scale
    gain = lambda shape: 1.0 + nrm(shape, 0.1)
    return {
        "x": nrm((BATCH, SEQ, D_MODEL), 1.0),
        "p": nrm((DEPTH, BATCH, SEQ, PLE_DIM), 1.0),
        "ffn1_norm_pre": gain((DEPTH, D_MODEL)),
        "ffn1_w_in": nrm((DEPTH, D_MODEL, 2 * D_FF), D_MODEL ** -0.5),
        "ffn1_w_out": nrm((DEPTH, D_FF, D_MODEL), D_FF ** -0.5),
        "ffn1_norm_post": gain((DEPTH, D_MODEL)),
        "mix_norm_pre": gain((DEPTH, D_MODEL)),
        "mix_norm_post": gain((DEPTH, D_MODEL)),
        "ffn2_norm_pre": gain((DEPTH, D_MODEL)),
        "ffn2_w_in": nrm((DEPTH, D_MODEL, 2 * D_FF), D_MODEL ** -0.5),
        "ffn2_w_out": nrm((DEPTH, D_FF, D_MODEL), D_FF ** -0.5),
        "ffn2_norm_post": gain((DEPTH, D_MODEL)),
        "hgrn_w_in": nrm((N_A_LAYERS, D_MODEL, 4 * HGRN_WIDTH), D_MODEL ** -0.5),
        "hgrn_lb_logits": nrm((N_A_LAYERS + 1, HGRN_WIDTH), 0.5),
        "hgrn_out_norm": gain((N_A_LAYERS, HGRN_DV)),
        "hgrn_w_out": nrm((N_A_LAYERS, HGRN_WIDTH, D_MODEL), HGRN_WIDTH ** -0.5),
        "kv_norm": gain((D_MODEL,)),
        "fox_w_kvf": nrm((D_MODEL, 2 * FOX_WIDTH + FOX_HEADS), D_MODEL ** -0.5),
        "fox_b_f": FOX_GATE_BIAS_OFFSET + nrm((FOX_HEADS,), 0.1),
        "fox_w_qg": nrm((N_B_LAYERS, D_MODEL, 2 * FOX_WIDTH), D_MODEL ** -0.5),
        "fox_w_out": nrm((N_B_LAYERS, FOX_WIDTH, D_MODEL), FOX_WIDTH ** -0.5),
        "ple_norm_pre": gain((DEPTH, D_MODEL)),
        "ple_w_gate": nrm((DEPTH, D_MODEL, D_MODEL), D_MODEL ** -0.5),
        "ple_w_proj": nrm((DEPTH, PLE_DIM, D_MODEL), PLE_DIM ** -0.5),
        "ple_norm_post": gain((DEPTH, D_MODEL)),
    }


def reference(x, p, ffn1_norm_pre, ffn1_w_in, ffn1_w_out, ffn1_norm_post,
              mix_norm_pre, mix_norm_post,
              ffn2_norm_pre, ffn2_w_in, ffn2_w_out, ffn2_norm_post,
              hgrn_w_in, hgrn_lb_logits, hgrn_out_norm, hgrn_w_out,
              kv_norm, fox_w_kvf, fox_b_f, fox_w_qg, fox_w_out,
              ple_norm_pre, ple_w_gate, ple_w_proj, ple_norm_post):
    lb_all = jnp.cumsum(jax.nn.softmax(hgrn_lb_logits.astype(jnp.float32), axis=0), axis=0)
    h = x
    k_sh = v_sh = c_sh = None
    for i in range(DEPTH):
        h = h + 0.5 * rmsnorm(swiglu(rmsnorm(h, ffn1_norm_pre[i]), ffn1_w_in[i], ffn1_w_out[i]), ffn1_norm_post[i])
        hn = rmsnorm(h, mix_norm_pre[i])
        if i < N_A_LAYERS:
            mix = hgrn2_mixer(hn, hgrn_w_in[i], lb_all[i], hgrn_out_norm[i], hgrn_w_out[i])
        else:
            j = i - N_A_LAYERS
            mix = fox_mixer(hn, fox_w_qg[j], fox_w_out[j], k_sh, v_sh, c_sh)
        h = h + rmsnorm(mix, mix_norm_post[i])
        h = h + 0.5 * rmsnorm(swiglu(rmsnorm(h, ffn2_norm_pre[i]), ffn2_w_in[i], ffn2_w_out[i]), ffn2_norm_post[i])
        gate = jax.nn.sigmoid(rmsnorm(h, ple_norm_pre[i]) @ ple_w_gate[i])
        h = h + rmsnorm(gate * (p[i] @ ple_w_proj[i]), ple_norm_post[i])
        if i == N_A_LAYERS - 1:
            k_sh, v_sh, c_sh = shared_kv(h, kv_norm, fox_w_kvf, fox_b_f)
    return h
```

```python
import functools

import numpy as np
import jax
import jax.numpy as jnp
from jax import lax
from jax.experimental import pallas as pl
from jax.experimental.pallas import tpu as pltpu

D_MODEL = 1024
BATCH = 8
SEQ = 4096
D_FF = 2816
PLE_DIM = 256
NORM_EPS = 1e-6
HGRN_HEADS = 8
HGRN_DK = 128
FOX_HEADS = 16
FOX_HEAD_DIM = 64
FOX_PAIRS = FOX_HEADS // 2

SUBLANES = 8
LANES = 128
MXU_DIM = 256

ROW_TILE = 512
FF_CHUNK = MXU_DIM
HGRN_CHUNK = 128
ATT_BLOCK = 512
VMEM_LIMIT = 56 * 1024 * 1024

BF16 = jnp.bfloat16
F32 = jnp.float32


def _rms(x, g):
    return x * lax.rsqrt(jnp.mean(x * x, axis=-1, keepdims=True) + NORM_EPS) * g


def _dot(a, b):
    return jnp.dot(a, b, preferred_element_type=F32)


def _dot_nt(a, b):
    return lax.dot_general(a, b, (((1,), (1,)), ((), ())), preferred_element_type=F32)


def _dot_tn(a, b):
    return lax.dot_general(a, b, (((0,), (0,)), ((), ())), preferred_element_type=F32)


def _resident(shape):
    zeros = (0,) * len(shape)
    return pl.BlockSpec(shape, lambda *_: zeros, pipeline_mode=pl.Buffered(1))


def _split3(x):
    h1 = x.astype(BF16)
    r1 = x - h1.astype(F32)
    h2 = r1.astype(BF16)
    h3 = (r1 - h2.astype(F32)).astype(BF16)
    return h1, h2, h3


def _ffn_kernel(x_ref, gpre_ref, win_ref, wout_ref, gpost_ref, o_ref, act_ref):
    x = x_ref[...]
    xn = _rms(x, gpre_ref[...]).astype(BF16)
    for j in range(D_FF // FF_CHUNK):
        lo = j * FF_CHUNK
        gate = _dot(xn, win_ref[:, lo:lo + FF_CHUNK])
        up = _dot(xn, win_ref[:, D_FF + lo:D_FF + lo + FF_CHUNK])
        act_ref[:, lo:lo + FF_CHUNK] = (gate * jax.nn.sigmoid(gate) * up).astype(BF16)
    y = _dot(act_ref[...], wout_ref[...])
    o_ref[...] = x + 0.5 * _rms(y, gpost_ref[...])


def _ffn(h2d, gpre, win, wout, gpost):
    rows = h2d.shape[0]
    row_spec = pl.BlockSpec((ROW_TILE, D_MODEL), lambda i: (i, 0))
    return pl.pallas_call(
        _ffn_kernel,
        out_shape=jax.ShapeDtypeStruct((rows, D_MODEL), F32),
        grid=(rows // ROW_TILE,),
        in_specs=[row_spec, _resident((1, D_MODEL)), _resident((D_MODEL, 2 * D_FF)),
                  _resident((D_FF, D_MODEL)), _resident((1, D_MODEL))],
        out_specs=row_spec,
        scratch_shapes=[pltpu.VMEM((ROW_TILE, D_FF), BF16)],
        compiler_params=pltpu.CompilerParams(
            dimension_semantics=("arbitrary",), vmem_limit_bytes=VMEM_LIMIT),
        name="ffn",
    )(h2d, gpre, win, wout, gpost)


def _level_table(c):
    t = np.arange(c)[:, None]
    s = np.arange(c)[None, :]
    x = np.maximum(t ^ s, 1)
    lv = np.floor(np.log2(x)).astype(np.int32)
    lv = np.where(s == t, -1, lv)
    lv = np.where(s > t, -2, lv)
    return lv.astype(np.int32)


def _hgrn_chunk(q, z, v, lb, oml, st, lv, tril):
    c = HGRN_CHUNK
    nt = c // SUBLANES
    e = jnp.exp(-jnp.abs(z))
    r = 1.0 / (1.0 + e)
    er = e * r
    pos = z >= 0
    f = lb + oml * jnp.where(pos, r, er)
    kk = oml * jnp.where(pos, er, r)
    lf = jnp.log(f)

    h1, h2, h3 = _split3(lf)
    cum = _dot(tril, h1) + _dot(tril, h2) + _dot(tril, h3)

    sub = lax.broadcasted_iota(jnp.int32, (SUBLANES, LANES), 0)
    zero = jnp.zeros((SUBLANES, LANES), F32)
    tiles = [cum[SUBLANES * j:SUBLANES * (j + 1), :] for j in range(nt)]

    def row(j, rr):
        return jnp.broadcast_to(cum[SUBLANES * j + rr:SUBLANES * j + rr + 1, :], (SUBLANES, LANES))

    b1 = [row(j, 1) for j in range(nt)]
    b3 = [row(j, 3) for j in range(nt)]
    b5 = [row(j, 5) for j in range(nt)]
    b7 = [row(j, 7) for j in range(nt)]
    prev7 = [zero] + b7[:-1]
    qt = [q[SUBLANES * j:SUBLANES * (j + 1), :] for j in range(nt)]
    kt = [kk[SUBLANES * j:SUBLANES * (j + 1), :] for j in range(nt)]

    def cat(ts):
        return jnp.concatenate(ts, axis=0).astype(BF16)

    v_bf = v.astype(BF16)
    kk_bf = kk.astype(BF16)
    sc = jnp.where(lv == -1, _dot_nt(q.astype(BF16), kk_bf), 0.0)
    sc = jnp.where(lv == 0, _dot_nt((q * f).astype(BF16), kk_bf), sc)

    level = 1
    blk = 2
    while blk < c:
        qs, ks = [], []
        for j in range(nt):
            if blk == 2:
                p = jnp.where(sub < 2, prev7[j], jnp.where(sub < 4, b1[j], jnp.where(sub < 6, b3[j], b5[j])))
                n = jnp.where(sub < 2, b1[j], jnp.where(sub < 4, b3[j], jnp.where(sub < 6, b5[j], b7[j])))
                use_q = use_k = True
            elif blk == 4:
                p = jnp.where(sub < 4, prev7[j], b3[j])
                n = jnp.where(sub < 4, b3[j], b7[j])
                use_q = use_k = True
            else:
                m = blk // SUBLANES
                b = j // m
                p = b7[b * m - 1] if b * m >= 1 else zero
                n = b7[b * m + m - 1]
                use_q = (b % 2) == 1
                use_k = (b % 2) == 0
            qs.append(qt[j] * jnp.exp(tiles[j] - p) if use_q else zero)
            ks.append(kt[j] * jnp.exp(n - tiles[j]) if use_k else zero)
        sc = jnp.where(lv == level, _dot_nt(cat(qs), cat(ks)), sc)
        level += 1
        blk *= 2

    last = b7[nt - 1]
    qe = cat([qt[j] * jnp.exp(tiles[j]) for j in range(nt)])
    ke = cat([kt[j] * jnp.exp(last - tiles[j]) for j in range(nt)])
    o = _dot_nt(qe, st.astype(BF16)) + _dot(sc.astype(BF16), v_bf)
    st_new = st * jnp.exp(last[0:1, :]) + _dot_tn(v_bf, ke)
    return o, st_new


def _hgrn_kernel(h_ref, gpre_ref, w_ref, lbl_ref, onorm_ref, wout_ref, gpost_ref, lv_ref, tril_ref,
                 o_ref, xn_ref, p_ref, og_ref, st_ref):
    c = HGRN_CHUNK

    @pl.when(pl.program_id(0) == 0)
    def _():
        st_ref[...] = jnp.zeros_like(st_ref)

    gpre = gpre_ref[...]
    for b in range(BATCH):
        xn_ref[b * c:(b + 1) * c, :] = _rms(h_ref[b], gpre).astype(BF16)

    l0 = lbl_ref[0:1, :]
    l1 = lbl_ref[1:2, :]
    mx = jnp.maximum(l0, l1)
    e0 = jnp.exp(l0 - mx)
    e1 = jnp.exp(l1 - mx)
    lb_all = e0 / (e0 + e1)
    onorm = onorm_ref[...]

    for head in range(HGRN_HEADS):
        p_ref[...] = _dot(xn_ref[...], w_ref[head])
        lb = lb_all[:, head * HGRN_DK:(head + 1) * HGRN_DK]
        oml = 1.0 - lb

        def body(b, carry, head=head, lb=lb, oml=oml):
            r0 = pl.multiple_of(b * c, c)
            blk = p_ref[pl.ds(r0, c), :]
            q = blk[:, 0:HGRN_DK]
            z = blk[:, HGRN_DK:2 * HGRN_DK]
            v = blk[:, 2 * HGRN_DK:3 * HGRN_DK]
            gg = blk[:, 3 * HGRN_DK:4 * HGRN_DK]
            idx = head * BATCH + b
            o, st_new = _hgrn_chunk(q, z, v, lb, oml, st_ref[idx], lv_ref[...], tril_ref[...])
            st_ref[idx] = st_new
            on = _rms(o, onorm)
            og_ref[pl.ds(r0, c), head * HGRN_DK:(head + 1) * HGRN_DK] = (
                on * (gg * jax.nn.sigmoid(gg))).astype(BF16)
            return carry

        lax.fori_loop(0, BATCH, body, 0, unroll=2)

    y = _dot(og_ref[...], wout_ref[...])
    gpost = gpost_ref[...]
    for b in range(BATCH):
        o_ref[b] = h_ref[b] + _rms(y[b * c:(b + 1) * c, :], gpost)


def _hgrn(h, gpre, w_heads, lb_logits, onorm, wout, gpost):
    c = HGRN_CHUNK
    lv = jnp.asarray(_level_table(c))
    tril = jnp.asarray(np.tril(np.ones((c, c), np.float32)), dtype=BF16)
    blk = pl.BlockSpec((BATCH, c, D_MODEL), lambda j: (0, j, 0))
    return pl.pallas_call(
        _hgrn_kernel,
        out_shape=jax.ShapeDtypeStruct((BATCH, SEQ, D_MODEL), F32),
        grid=(SEQ // c,),
        in_specs=[blk, _resident((1, D_MODEL)), _resident((HGRN_HEADS, D_MODEL, 4 * HGRN_DK)),
                  _resident((2, D_MODEL)), _resident((1, HGRN_DK)), _resident((D_MODEL, D_MODEL)),
                  _resident((1, D_MODEL)), _resident((c, c)), _resident((c, c))],
        out_specs=blk,
        scratch_shapes=[pltpu.VMEM((BATCH * c, D_MODEL), BF16),
                        pltpu.VMEM((BATCH * c, 4 * HGRN_DK), F32),
                        pltpu.VMEM((BATCH * c, D_MODEL), BF16),
                        pltpu.VMEM((HGRN_HEADS * BATCH, HGRN_DK, HGRN_DK), F32)],
        compiler_params=pltpu.CompilerParams(
            dimension_semantics=("arbitrary",), vmem_limit_bytes=VMEM_LIMIT),
        name="hgrn",
    )(h, gpre, w_heads, lb_logits, onorm, wout, gpost, lv, tril)


def _ple_body(h_ref, p_ref, gpre_ref, wg_ref, wp_ref, gpost_ref):
    x = h_ref[0]
    xn = _rms(x, gpre_ref[...]).astype(BF16)
    gate = jax.nn.sigmoid(_dot(xn, wg_ref[...]))
    pp = _dot(p_ref[0].astype(BF16), wp_ref[...])
    return x + _rms(gate * pp, gpost_ref[...])


def _ple_kernel(h_ref, p_ref, gpre_ref, wg_ref, wp_ref, gpost_ref, o_ref):
    o_ref[0] = _ple_body(h_ref, p_ref, gpre_ref, wg_ref, wp_ref, gpost_ref)


def _ple_kv_kernel(h_ref, p_ref, gpre_ref, wg_ref, wp_ref, gpost_ref,
                   kvn_ref, wkv_ref, wft_ref, bf_ref, triu_ref,
                   o_ref, k_ref, v_ref, c_ref, carry_ref):
    hn = _ple_body(h_ref, p_ref, gpre_ref, wg_ref, wp_ref, gpost_ref)
    o_ref[0] = hn
    xn = _rms(hn, kvn_ref[...]).astype(BF16)
    kv = _dot(xn, wkv_ref[...])
    k_ref[0] = kv[:, :D_MODEL].astype(BF16)
    v_ref[0] = kv[:, D_MODEL:].astype(BF16)

    @pl.when(pl.program_id(1) == 0)
    def _():
        carry_ref[...] = jnp.zeros_like(carry_ref)

    x = _dot_nt(wft_ref[...], xn) + bf_ref[...]
    ls = jnp.minimum(x, 0.0) - jnp.log1p(jnp.exp(-jnp.abs(x)))
    h1, h2, h3 = _split3(ls)
    triu = triu_ref[...]
    cum = _dot(h1, triu) + _dot(h2, triu) + _dot(h3, triu) + carry_ref[:, 0:1]
    for pair in range(FOX_PAIRS):
        c_ref[0, 0, pair] = cum[2 * pair:2 * pair + 2, :]
    carry_ref[...] = jnp.broadcast_to(cum[:, ROW_TILE - 1:ROW_TILE], carry_ref.shape)


def _ple(h, p, layer, gpre, wg, wp, gpost):
    hs = pl.BlockSpec((1, ROW_TILE, D_MODEL), lambda b, i: (b, i, 0))
    ps = pl.BlockSpec((1, ROW_TILE, PLE_DIM), lambda b, i: (layer * BATCH + b, i, 0))
    return pl.pallas_call(
        _ple_kernel,
        out_shape=jax.ShapeDtypeStruct((BATCH, SEQ, D_MODEL), F32),
        grid=(BATCH, SEQ // ROW_TILE),
        in_specs=[hs, ps, _resident((1, D_MODEL)), _resident((D_MODEL, D_MODEL)),
                  _resident((PLE_DIM, D_MODEL)), _resident((1, D_MODEL))],
        out_specs=hs,
        compiler_params=pltpu.CompilerParams(
            dimension_semantics=("arbitrary", "arbitrary"), vmem_limit_bytes=VMEM_LIMIT),
        name="ple",
    )(h, p, gpre, wg, wp, gpost)


def _ple_kv(h, p, gpre, wg, wp, gpost, kvn, wkv, wft, bf):
    assert ROW_TILE == ATT_BLOCK
    nblk = SEQ // ROW_TILE
    hs = pl.BlockSpec((1, ROW_TILE, D_MODEL), lambda b, i: (b, i, 0))
    ps = pl.BlockSpec((1, ROW_TILE, PLE_DIM), lambda b, i: (b, i, 0))
    cs = pl.BlockSpec((1, 1, FOX_PAIRS, 2, ROW_TILE), lambda b, i: (b, i, 0, 0, 0))
    triu = jnp.asarray(np.triu(np.ones((ROW_TILE, ROW_TILE), np.float32)), dtype=BF16)
    return pl.pallas_call(
        _ple_kv_kernel,
        out_shape=(jax.ShapeDtypeStruct((BATCH, SEQ, D_MODEL), F32),
                   jax.ShapeDtypeStruct((BATCH, SEQ, D_MODEL), BF16),
                   jax.ShapeDtypeStruct((BATCH, SEQ, D_MODEL), BF16),
                   jax.ShapeDtypeStruct((BATCH, nblk, FOX_PAIRS, 2, ROW_TILE), F32)),
        grid=(BATCH, nblk),
        in_specs=[hs, ps, _resident((1, D_MODEL)), _resident((D_MODEL, D_MODEL)),
                  _resident((PLE_DIM, D_MODEL)), _resident((1, D_MODEL)),
                  _resident((1, D_MODEL)), _resident((D_MODEL, 2 * D_MODEL)),
                  _resident((FOX_HEADS, D_MODEL)), _resident((FOX_HEADS, 1)),
                  _resident((ROW_TILE, ROW_TILE))],
        out_specs=(hs, hs, hs, cs),
        scratch_shapes=[pltpu.VMEM((FOX_HEADS, LANES), F32)],
        compiler_params=pltpu.CompilerParams(
            dimension_semantics=("arbitrary", "arbitrary"), vmem_limit_bytes=VMEM_LIMIT),
        name="ple_kv",
    )(h, p, gpre, wg, wp, gpost, kvn, wkv, wft, bf, triu)


def _qg_kernel(h_ref, gpre_ref, w_ref, q_ref, g_ref):
    xn = _rms(h_ref[...], gpre_ref[...]).astype(BF16)
    qg = _dot(xn, w_ref[...])
    q_ref[...] = (qg[:, :D_MODEL] * (FOX_HEAD_DIM ** -0.5)).astype(BF16)
    g_ref[...] = jax.nn.sigmoid(qg[:, D_MODEL:])


def _qg(h2d, gpre, w):
    rows = h2d.shape[0]
    row_spec = pl.BlockSpec((ROW_TILE, D_MODEL), lambda i: (i, 0))
    return pl.pallas_call(
        _qg_kernel,
        out_shape=(jax.ShapeDtypeStruct((rows, D_MODEL), BF16),
                   jax.ShapeDtypeStruct((rows, D_MODEL), F32)),
        grid=(rows // ROW_TILE,),
        in_specs=[row_spec, _resident((1, D_MODEL)), _resident((D_MODEL, 2 * D_MODEL))],
        out_specs=(row_spec, row_spec),
        compiler_params=pltpu.CompilerParams(
            dimension_semantics=("arbitrary",), vmem_limit_bytes=VMEM_LIMIT),
        name="fox_qg",
    )(h2d, gpre, w)


def _attn_kernel(q_ref, k_ref, v_ref, c_ref, g_ref, o_ref):
    t = ATT_BLOCK
    qi = pl.program_id(2)
    q2 = q_ref[0].astype(F32)
    lane = lax.broadcasted_iota(jnp.int32, (t, LANES), 1)
    rows = lax.broadcasted_iota(jnp.int32, (t, t), 0)
    cols = lax.broadcasted_iota(jnp.int32, (t, t), 1)
    eye = rows == cols
    causal = cols <= rows
    outs = []
    for hd in range(2):
        in_head = (lane >= hd * FOX_HEAD_DIM) & (lane < (hd + 1) * FOX_HEAD_DIM)
        qm = jnp.where(in_head, q2, 0.0).astype(BF16)
        c_row_q = c_ref[0, qi, 0, hd:hd + 1, :]
        c_t = jnp.sum(jnp.where(eye, jnp.broadcast_to(c_row_q, (t, t)), 0.0), axis=1, keepdims=True)

        def step(j, carry, masked, qm=qm, c_t=c_t, hd=hd):
            m, l, acc = carry
            r0 = pl.multiple_of(j * t, t)
            kb = k_ref[0, pl.ds(r0, t), :]
            vb = v_ref[0, pl.ds(r0, t), :]
            u = _dot_nt(qm, kb) - c_ref[0, j, 0, hd:hd + 1, :]
            if masked:
                u = jnp.where(causal, u, -1e30)
            m_new = jnp.maximum(m, jnp.max(u, axis=1, keepdims=True) + c_t)
            p = jnp.exp(u + (c_t - m_new))
            alpha = jnp.exp(m - m_new)
            l = alpha * l + jnp.sum(p, axis=1, keepdims=True)
            acc = alpha * acc + _dot(p.astype(BF16), vb)
            return m_new, l, acc

        init = (jnp.full((t, 1), -1e30, F32), jnp.zeros((t, 1), F32), jnp.zeros((t, LANES), F32))
        carry = lax.fori_loop(0, qi, functools.partial(step, masked=False), init)
        m, l, acc = step(qi, carry, True)
        outs.append(acc / l)
    o = jnp.where(lane < FOX_HEAD_DIM, outs[0], outs[1])
    o_ref[0] = (o * g_ref[0]).astype(BF16)


def _attn(q, k, v, c, g):
    t = ATT_BLOCK
    nblk = SEQ // t
    qs = pl.BlockSpec((1, t, LANES), lambda b, hp, i: (b, i, hp))
    kvs = pl.BlockSpec((1, SEQ, LANES), lambda b, hp, i: (b, 0, hp))
    cs = pl.BlockSpec((1, nblk, 1, 2, t), lambda b, hp, i: (b, 0, hp, 0, 0))
    return pl.pallas_call(
        _attn_kernel,
        out_shape=jax.ShapeDtypeStruct((BATCH, SEQ, D_MODEL), BF16),
        grid=(BATCH, FOX_PAIRS, nblk),
        in_specs=[qs, kvs, kvs, cs, qs],
        out_specs=qs,
        compiler_params=pltpu.CompilerParams(
            dimension_semantics=("arbitrary", "arbitrary", "arbitrary"), vmem_limit_bytes=VMEM_LIMIT),
        name="fox_attn",
    )(q, k, v, c, g)


def _proj_res_kernel(h_ref, a_ref, w_ref, gpost_ref, o_ref):
    y = _dot(a_ref[...], w_ref[...])
    o_ref[...] = h_ref[...] + _rms(y, gpost_ref[...])


def _proj_res(h2d, a2d, w, gpost):
    rows = h2d.shape[0]
    row_spec = pl.BlockSpec((ROW_TILE, D_MODEL), lambda i: (i, 0))
    return pl.pallas_call(
        _proj_res_kernel,
        out_shape=jax.ShapeDtypeStruct((rows, D_MODEL), F32),
        grid=(rows // ROW_TILE,),
        in_specs=[row_spec, row_spec, _resident((D_MODEL, D_MODEL)), _resident((1, D_MODEL))],
        out_specs=row_spec,
        compiler_params=pltpu.CompilerParams(
            dimension_semantics=("arbitrary",), vmem_limit_bytes=VMEM_LIMIT),
        name="fox_out",
    )(h2d, a2d, w, gpost)


def kernel(x, p, ffn1_norm_pre, ffn1_w_in, ffn1_w_out, ffn1_norm_post, mix_norm_pre, mix_norm_post, ffn2_norm_pre, ffn2_w_in, ffn2_w_out, ffn2_norm_post, hgrn_w_in, hgrn_lb_logits, hgrn_out_norm, hgrn_w_out, kv_norm, fox_w_kvf, fox_b_f, fox_w_qg, fox_w_out, ple_norm_pre, ple_w_gate, ple_w_proj, ple_norm_post):
    assert x.shape == (BATCH, SEQ, D_MODEL) and p.shape == (2, BATCH, SEQ, PLE_DIM)
    rows = BATCH * SEQ
    bf = lambda w: w.astype(BF16)
    vec = lambda g: g.reshape(1, -1).astype(F32)
    flat = lambda h: h.reshape(rows, D_MODEL)
    cube = lambda h: h.reshape(BATCH, SEQ, D_MODEL)
    p2 = p.reshape(2 * BATCH, SEQ, PLE_DIM)

    def ffn1(h2d, i):
        return _ffn(h2d, vec(ffn1_norm_pre[i]), bf(ffn1_w_in[i]), bf(ffn1_w_out[i]), vec(ffn1_norm_post[i]))

    def ffn2(h2d, i):
        return _ffn(h2d, vec(ffn2_norm_pre[i]), bf(ffn2_w_in[i]), bf(ffn2_w_out[i]), vec(ffn2_norm_post[i]))

    h = ffn1(flat(x), 0)
    w_heads = bf(hgrn_w_in[0].reshape(D_MODEL, 4, HGRN_HEADS, HGRN_DK).transpose(2, 0, 1, 3)
                 .reshape(HGRN_HEADS, D_MODEL, 4 * HGRN_DK))
    h = _hgrn(cube(h), vec(mix_norm_pre[0]), w_heads, hgrn_lb_logits.astype(F32),
              vec(hgrn_out_norm[0]), bf(hgrn_w_out[0]), vec(mix_norm_post[0]))
    h = ffn2(flat(h), 0)
    h, k_sh, v_sh, c_sh = _ple_kv(
        cube(h), p2, vec(ple_norm_pre[0]), bf(ple_w_gate[0]), bf(ple_w_proj[0]), vec(ple_norm_post[0]),
        vec(kv_norm), bf(fox_w_kvf[:, :2 * D_MODEL]), bf(fox_w_kvf[:, 2 * D_MODEL:].T),
        fox_b_f.reshape(FOX_HEADS, 1).astype(F32))

    h = ffn1(flat(h), 1)
    q, g = _qg(h, vec(mix_norm_pre[1]), bf(fox_w_qg[0]))
    a = _attn(cube(q), k_sh, v_sh, c_sh, cube(g))
    h = _proj_res(h, flat(a), bf(fox_w_out[0]), vec(mix_norm_post[1]))
    h = ffn2(h, 1)
    h = _ple(cube(h), p2, 1, vec(ple_norm_pre[1]), bf(ple_w_gate[1]), bf(ple_w_proj[1]), vec(ple_norm_post[1]))
    return h
```

```python
import functools

import numpy as np
import jax
import jax.numpy as jnp
from jax import lax
from jax.experimental import pallas as pl
from jax.experimental.pallas import tpu as pltpu

D_MODEL = 1024
BATCH = 8
SEQ = 4096
D_FF = 2816
PLE_DIM = 256
NORM_EPS = 1e-6
HGRN_HEADS = 8
HGRN_DK = 128
FOX_HEADS = 16
FOX_HEAD_DIM = 64
FOX_PAIRS = FOX_HEADS // 2

SUBLANES = 8
LANES = 128
MXU_DIM = 256

ROW_TILE = 512
FF_CHUNK = MXU_DIM
HGRN_CHUNK = 128
ATT_BLOCK = 512
BIAS_LANES = 8
VT_ROWS = FOX_HEAD_DIM + 16
LOG2E = 1.4426950408889634
VMEM_LIMIT = 56 * 1024 * 1024

BF16 = jnp.bfloat16
F32 = jnp.float32


def _rms(x, g):
    return x * lax.rsqrt(jnp.mean(x * x, axis=-1, keepdims=True) + NORM_EPS) * g


def _dot(a, b):
    return jnp.dot(a, b, preferred_element_type=F32)


def _dot_nt(a, b):
    return lax.dot_general(a, b, (((1,), (1,)), ((), ())), preferred_element_type=F32)


def _dot_tn(a, b):
    return lax.dot_general(a, b, (((0,), (0,)), ((), ())), preferred_element_type=F32)


def _resident(shape):
    zeros = (0,) * len(shape)
    return pl.BlockSpec(shape, lambda *_: zeros, pipeline_mode=pl.Buffered(1))


def _split3(x):
    h1 = x.astype(BF16)
    r1 = x - h1.astype(F32)
    h2 = r1.astype(BF16)
    h3 = (r1 - h2.astype(F32)).astype(BF16)
    return h1, h2, h3


def _ffn_kernel(x_ref, gpre_ref, win_ref, wout_ref, gpost_ref, o_ref, act_ref):
    x = x_ref[...]
    xn = _rms(x, gpre_ref[...]).astype(BF16)
    for j in range(D_FF // FF_CHUNK):
        lo = j * FF_CHUNK
        gate = _dot(xn, win_ref[:, lo:lo + FF_CHUNK])
        up = _dot(xn, win_ref[:, D_FF + lo:D_FF + lo + FF_CHUNK])
        act_ref[:, lo:lo + FF_CHUNK] = (gate * jax.nn.sigmoid(gate) * up).astype(BF16)
    y = _dot(act_ref[...], wout_ref[...])
    o_ref[...] = x + 0.5 * _rms(y, gpost_ref[...])


def _ffn(h2d, gpre, win, wout, gpost):
    rows = h2d.shape[0]
    row_spec = pl.BlockSpec((ROW_TILE, D_MODEL), lambda i: (i, 0))
    return pl.pallas_call(
        _ffn_kernel,
        out_shape=jax.ShapeDtypeStruct((rows, D_MODEL), F32),
        grid=(rows // ROW_TILE,),
        in_specs=[row_spec, _resident((1, D_MODEL)), _resident((D_MODEL, 2 * D_FF)),
                  _resident((D_FF, D_MODEL)), _resident((1, D_MODEL))],
        out_specs=row_spec,
        scratch_shapes=[pltpu.VMEM((ROW_TILE, D_FF), BF16)],
        compiler_params=pltpu.CompilerParams(
            dimension_semantics=("arbitrary",), vmem_limit_bytes=VMEM_LIMIT),
        name="ffn",
    )(h2d, gpre, win, wout, gpost)


def _level_table(c):
    t = np.arange(c)[:, None]
    s = np.arange(c)[None, :]
    x = np.maximum(t ^ s, 1)
    lv = np.floor(np.log2(x)).astype(np.int32)
    lv = np.where(s == t, -1, lv)
    lv = np.where(s > t, -2, lv)
    return lv.astype(np.int32)


def _hgrn_chunk(q, z, v, lb, oml, st, lv, tril):
    c = HGRN_CHUNK
    nt = c // SUBLANES
    e = jnp.exp(-jnp.abs(z))
    r = 1.0 / (1.0 + e)
    er = e * r
    pos = z >= 0
    f = lb + oml * jnp.where(pos, r, er)
    kk = oml * jnp.where(pos, er, r)
    lf = jnp.log(f)

    h1, h2, h3 = _split3(lf)
    cum = _dot(tril, h1) + _dot(tril, h2) + _dot(tril, h3)

    sub = lax.broadcasted_iota(jnp.int32, (SUBLANES, LANES), 0)
    zero = jnp.zeros((SUBLANES, LANES), F32)
    tiles = [cum[SUBLANES * j:SUBLANES * (j + 1), :] for j in range(nt)]

    def row(j, rr):
        return jnp.broadcast_to(cum[SUBLANES * j + rr:SUBLANES * j + rr + 1, :], (SUBLANES, LANES))

    b1 = [row(j, 1) for j in range(nt)]
    b3 = [row(j, 3) for j in range(nt)]
    b5 = [row(j, 5) for j in range(nt)]
    b7 = [row(j, 7) for j in range(nt)]
    prev7 = [zero] + b7[:-1]
    qt = [q[SUBLANES * j:SUBLANES * (j + 1), :] for j in range(nt)]
    kt = [kk[SUBLANES * j:SUBLANES * (j + 1), :] for j in range(nt)]

    def cat(ts):
        return jnp.concatenate(ts, axis=0).astype(BF16)

    v_bf = v.astype(BF16)
    kk_bf = kk.astype(BF16)
    sc = jnp.where(lv == -1, _dot_nt(q.astype(BF16), kk_bf), 0.0)
    sc = jnp.where(lv == 0, _dot_nt((q * f).astype(BF16), kk_bf), sc)

    level = 1
    blk = 2
    while blk < c:
        qs, ks = [], []
        for j in range(nt):
            if blk == 2:
                p = jnp.where(sub < 2, prev7[j], jnp.where(sub < 4, b1[j], jnp.where(sub < 6, b3[j], b5[j])))
                n = jnp.where(sub < 2, b1[j], jnp.where(sub < 4, b3[j], jnp.where(sub < 6, b5[j], b7[j])))
                use_q = use_k = True
            elif blk == 4:
                p = jnp.where(sub < 4, prev7[j], b3[j])
                n = jnp.where(sub < 4, b3[j], b7[j])
                use_q = use_k = True
            else:
                m = blk // SUBLANES
                b = j // m
                p = b7[b * m - 1] if b * m >= 1 else zero
                n = b7[b * m + m - 1]
                use_q = (b % 2) == 1
                use_k = (b % 2) == 0
            qs.append(qt[j] * jnp.exp(tiles[j] - p) if use_q else zero)
            ks.append(kt[j] * jnp.exp(n - tiles[j]) if use_k else zero)
        sc = jnp.where(lv == level, _dot_nt(cat(qs), cat(ks)), sc)
        level += 1
        blk *= 2

    last = b7[nt - 1]
    qe = cat([qt[j] * jnp.exp(tiles[j]) for j in range(nt)])
    ke = cat([kt[j] * jnp.exp(last - tiles[j]) for j in range(nt)])
    o = _dot_nt(qe, st.astype(BF16)) + _dot(sc.astype(BF16), v_bf)
    st_new = st * jnp.exp(last[0:1, :]) + _dot_tn(v_bf, ke)
    return o, st_new


def _hgrn_kernel(h_ref, gpre_ref, w_ref, lbl_ref, onorm_ref, wout_ref, gpost_ref, lv_ref, tril_ref,
                 o_ref, xn_ref, p_ref, og_ref, st_ref):
    c = HGRN_CHUNK

    @pl.when(pl.program_id(0) == 0)
    def _():
        st_ref[...] = jnp.zeros_like(st_ref)

    gpre = gpre_ref[...]
    for b in range(BATCH):
        xn_ref[b * c:(b + 1) * c, :] = _rms(h_ref[b], gpre).astype(BF16)

    l0 = lbl_ref[0:1, :]
    l1 = lbl_ref[1:2, :]
    mx = jnp.maximum(l0, l1)
    e0 = jnp.exp(l0 - mx)
    e1 = jnp.exp(l1 - mx)
    lb_all = e0 / (e0 + e1)
    onorm = onorm_ref[...]

    for head in range(HGRN_HEADS):
        p_ref[...] = _dot(xn_ref[...], w_ref[head])
        lb = lb_all[:, head * HGRN_DK:(head + 1) * HGRN_DK]
        oml = 1.0 - lb

        def body(b, carry, head=head, lb=lb, oml=oml):
            r0 = pl.multiple_of(b * c, c)
            blk = p_ref[pl.ds(r0, c), :]
            q = blk[:, 0:HGRN_DK]
            z = blk[:, HGRN_DK:2 * HGRN_DK]
            v = blk[:, 2 * HGRN_DK:3 * HGRN_DK]
            gg = blk[:, 3 * HGRN_DK:4 * HGRN_DK]
            idx = head * BATCH + b
            o, st_new = _hgrn_chunk(q, z, v, lb, oml, st_ref[idx], lv_ref[...], tril_ref[...])
            st_ref[idx] = st_new
            on = _rms(o, onorm)
            og_ref[pl.ds(r0, c), head * HGRN_DK:(head + 1) * HGRN_DK] = (
                on * (gg * jax.nn.sigmoid(gg))).astype(BF16)
            return carry

        lax.fori_loop(0, BATCH, body, 0, unroll=2)

    y = _dot(og_ref[...], wout_ref[...])
    gpost = gpost_ref[...]
    for b in range(BATCH):
        o_ref[b] = h_ref[b] + _rms(y[b * c:(b + 1) * c, :], gpost)


def _hgrn(h, gpre, w_heads, lb_logits, onorm, wout, gpost):
    c = HGRN_CHUNK
    lv = jnp.asarray(_level_table(c))
    tril = jnp.asarray(np.tril(np.ones((c, c), np.float32)), dtype=BF16)
    blk = pl.BlockSpec((BATCH, c, D_MODEL), lambda j: (0, j, 0))
    return pl.pallas_call(
        _hgrn_kernel,
        out_shape=jax.ShapeDtypeStruct((BATCH, SEQ, D_MODEL), F32),
        grid=(SEQ // c,),
        in_specs=[blk, _resident((1, D_MODEL)), _resident((HGRN_HEADS, D_MODEL, 4 * HGRN_DK)),
                  _resident((2, D_MODEL)), _resident((1, HGRN_DK)), _resident((D_MODEL, D_MODEL)),
                  _resident((1, D_MODEL)), _resident((c, c)), _resident((c, c))],
        out_specs=blk,
        scratch_shapes=[pltpu.VMEM((BATCH * c, D_MODEL), BF16),
                        pltpu.VMEM((BATCH * c, 4 * HGRN_DK), F32),
                        pltpu.VMEM((BATCH * c, D_MODEL), BF16),
                        pltpu.VMEM((HGRN_HEADS * BATCH, HGRN_DK, HGRN_DK), F32)],
        compiler_params=pltpu.CompilerParams(
            dimension_semantics=("arbitrary",), vmem_limit_bytes=VMEM_LIMIT),
        name="hgrn",
    )(h, gpre, w_heads, lb_logits, onorm, wout, gpost, lv, tril)


def _ple_body(h_ref, p_ref, gpre_ref, wg_ref, wp_ref, gpost_ref):
    x = h_ref[0]
    xn = _rms(x, gpre_ref[...]).astype(BF16)
    gate = jax.nn.sigmoid(_dot(xn, wg_ref[...]))
    pp = _dot(p_ref[0].astype(BF16), wp_ref[...])
    return x + _rms(gate * pp, gpost_ref[...])


def _ple_kernel(h_ref, p_ref, gpre_ref, wg_ref, wp_ref, gpost_ref, o_ref):
    o_ref[0] = _ple_body(h_ref, p_ref, gpre_ref, wg_ref, wp_ref, gpost_ref)


def _ple_kv_kernel(h_ref, p_ref, gpre_ref, wg_ref, wp_ref, gpost_ref,
                   kvn_ref, wk_ref, wvt_ref, wf_ref, bf_ref, tril_ref, ek_ref, eq_ref,
                   o_ref, k_ref, kx_ref, qx_ref, vt_ref, carry_ref):
    hn = _ple_body(h_ref, p_ref, gpre_ref, wg_ref, wp_ref, gpost_ref)
    o_ref[0] = hn
    xn = _rms(hn, kvn_ref[...]).astype(BF16)
    k_ref[0] = _dot(xn, wk_ref[...]).astype(BF16)
    vt = _dot_nt(wvt_ref[...], xn).astype(BF16)
    ones = jnp.ones((VT_ROWS - FOX_HEAD_DIM, ROW_TILE), BF16)
    for hd in range(FOX_HEADS):
        vt_ref[0, 0, hd, 0:FOX_HEAD_DIM, :] = vt[hd * FOX_HEAD_DIM:(hd + 1) * FOX_HEAD_DIM, :]
        vt_ref[0, 0, hd, FOX_HEAD_DIM:VT_ROWS, :] = ones

    @pl.when(pl.program_id(1) == 0)
    def _():
        carry_ref[...] = jnp.zeros_like(carry_ref)

    x = _dot(xn, wf_ref[...]) + bf_ref[...]
    ls = jnp.minimum(x, 0.0) - jnp.log1p(jnp.exp(-jnp.abs(x)))
    h1, h2, h3 = _split3(ls)
    tril = tril_ref[...]
    cum = _dot(tril, h1) + _dot(tril, h2) + _dot(tril, h3) + carry_ref[0:1, :]
    carry_ref[...] = jnp.broadcast_to(cum[ROW_TILE - 1:ROW_TILE, :], carry_ref.shape)
    c2 = cum * LOG2E
    hi = c2.astype(BF16).astype(F32)
    r1 = c2 - hi
    mid = r1.astype(BF16).astype(F32)
    lo = r1 - mid
    lane = lax.broadcasted_iota(jnp.int32, c2.shape, 1)
    cx = jnp.where(lane < FOX_HEADS, hi, jnp.where(lane < 2 * FOX_HEADS, mid, jnp.where(
        lane < 3 * FOX_HEADS, lo, jnp.where(lane == 3 * FOX_HEADS, 1.0, 0.0)))).astype(BF16)
    kx_ref[0] = _dot(cx, ek_ref[...]).astype(BF16)
    qx_ref[0] = _dot(cx, eq_ref[...]).astype(BF16)


def _ple(h, p, layer, gpre, wg, wp, gpost):
    hs = pl.BlockSpec((1, ROW_TILE, D_MODEL), lambda b, i: (b, i, 0))
    ps = pl.BlockSpec((1, ROW_TILE, PLE_DIM), lambda b, i: (layer * BATCH + b, i, 0))
    return pl.pallas_call(
        _ple_kernel,
        out_shape=jax.ShapeDtypeStruct((BATCH, SEQ, D_MODEL), F32),
        grid=(BATCH, SEQ // ROW_TILE),
        in_specs=[hs, ps, _resident((1, D_MODEL)), _resident((D_MODEL, D_MODEL)),
                  _resident((PLE_DIM, D_MODEL)), _resident((1, D_MODEL))],
        out_specs=hs,
        compiler_params=pltpu.CompilerParams(
            dimension_semantics=("arbitrary", "arbitrary"), vmem_limit_bytes=VMEM_LIMIT),
        name="ple",
    )(h, p, gpre, wg, wp, gpost)


def _bias_placement():
    ek = np.zeros((LANES, D_MODEL), np.float32)
    eq = np.zeros((LANES, D_MODEL), np.float32)
    one_lane = 3 * FOX_HEADS
    for head in range(FOX_HEADS):
        base = (head // 2) * LANES + (head % 2) * BIAS_LANES
        for i in range(3):
            ek[i * FOX_HEADS + head, base + i] = 1.0
            eq[one_lane, base + i] = -1.0
            ek[one_lane, base + 3 + i] = 1.0
            eq[i * FOX_HEADS + head, base + 3 + i] = 1.0
    return jnp.asarray(ek, dtype=BF16), jnp.asarray(eq, dtype=BF16)


def _ple_kv(h, p, gpre, wg, wp, gpost, kvn, wk, wvt, wf3, bf3):
    assert ROW_TILE == ATT_BLOCK
    nblk = SEQ // ROW_TILE
    hs = pl.BlockSpec((1, ROW_TILE, D_MODEL), lambda b, i: (b, i, 0))
    ps = pl.BlockSpec((1, ROW_TILE, PLE_DIM), lambda b, i: (b, i, 0))
    vts = pl.BlockSpec((1, 1, FOX_HEADS, VT_ROWS, ROW_TILE), lambda b, i: (b, i, 0, 0, 0))
    tril = jnp.asarray(np.tril(np.ones((ROW_TILE, ROW_TILE), np.float32)), dtype=BF16)
    ek, eq = _bias_placement()
    return pl.pallas_call(
        _ple_kv_kernel,
        out_shape=(jax.ShapeDtypeStruct((BATCH, SEQ, D_MODEL), F32),
                   jax.ShapeDtypeStruct((BATCH, SEQ, D_MODEL), BF16),
                   jax.ShapeDtypeStruct((BATCH, SEQ, D_MODEL), BF16),
                   jax.ShapeDtypeStruct((BATCH, SEQ, D_MODEL), BF16),
                   jax.ShapeDtypeStruct((BATCH, nblk, FOX_HEADS, VT_ROWS, ROW_TILE), BF16)),
        grid=(BATCH, nblk),
        in_specs=[hs, ps, _resident((1, D_MODEL)), _resident((D_MODEL, D_MODEL)),
                  _resident((PLE_DIM, D_MODEL)), _resident((1, D_MODEL)),
                  _resident((1, D_MODEL)), _resident((D_MODEL, D_MODEL)), _resident((D_MODEL, D_MODEL)),
                  _resident((D_MODEL, LANES)), _resident((1, LANES)),
                  _resident((ROW_TILE, ROW_TILE)), _resident((LANES, D_MODEL)), _resident((LANES, D_MODEL))],
        out_specs=(hs, hs, hs, hs, vts),
        scratch_shapes=[pltpu.VMEM((SUBLANES, LANES), F32)],
        compiler_params=pltpu.CompilerParams(
            dimension_semantics=("arbitrary", "arbitrary"), vmem_limit_bytes=VMEM_LIMIT),
        name="ple_kv",
    )(h, p, gpre, wg, wp, gpost, kvn, wk, wvt, wf3, bf3, tril, ek, eq)


def _qg_kernel(h_ref, gpre_ref, w_ref, q_ref, g_ref):
    xn = _rms(h_ref[...], gpre_ref[...]).astype(BF16)
    qg = _dot(xn, w_ref[...])
    q_ref[...] = (qg[:, :D_MODEL] * (FOX_HEAD_DIM ** -0.5 * LOG2E)).astype(BF16)
    g_ref[...] = jax.nn.sigmoid(qg[:, D_MODEL:])


def _qg(h2d, gpre, w):
    rows = h2d.shape[0]
    row_spec = pl.BlockSpec((ROW_TILE, D_MODEL), lambda i: (i, 0))
    return pl.pallas_call(
        _qg_kernel,
        out_shape=(jax.ShapeDtypeStruct((rows, D_MODEL), BF16),
                   jax.ShapeDtypeStruct((rows, D_MODEL), F32)),
        grid=(rows // ROW_TILE,),
        in_specs=[row_spec, _resident((1, D_MODEL)), _resident((D_MODEL, 2 * D_MODEL))],
        out_specs=(row_spec, row_spec),
        compiler_params=pltpu.CompilerParams(
            dimension_semantics=("arbitrary",), vmem_limit_bytes=VMEM_LIMIT),
        name="fox_qg",
    )(h2d, gpre, w)


def _attn_kernel(q_ref, qx_ref, k_ref, kx_ref, vt_ref, g_ref, o_ref, s_ref):
    t = ATT_BLOCK
    qi = pl.program_id(2)
    lane = lax.broadcasted_iota(jnp.int32, (t, LANES), 1)
    q2 = q_ref[0].astype(F32)
    qx2 = qx_ref[0].astype(F32)
    qcat = []
    for hd in range(2):
        in_head = (lane >= hd * FOX_HEAD_DIM) & (lane < (hd + 1) * FOX_HEAD_DIM)
        in_bias = (lane >= hd * BIAS_LANES) & (lane < (hd + 1) * BIAS_LANES)
        qcat.append(jnp.concatenate([jnp.where(in_head, q2, 0.0).astype(BF16),
                                     jnp.where(in_bias, qx2, 0.0).astype(BF16)], axis=1))
    key = lax.broadcasted_iota(jnp.int32, (t, t), 0)
    qry = lax.broadcasted_iota(jnp.int32, (t, t), 1)
    causal = key <= qry

    def produce(j, masked):
        r0 = pl.multiple_of(j * t, t)
        kcat = jnp.concatenate([k_ref[0, pl.ds(r0, t), :], kx_ref[0, pl.ds(r0, t), :]], axis=1)
        mb = []
        for hd in range(2):
            st = _dot_nt(kcat, qcat[hd])
            if masked:
                st = jnp.where(causal, st, -1e30)
            s_ref[hd] = st
            mb.append(jnp.max(st, axis=0, keepdims=True))
        return tuple(mb)

    def consume(vj, state, mb):
        out = []
        for hd in range(2):
            m, acc = state[hd]
            m_new = jnp.maximum(m, mb[hd])
            p = jnp.exp2(s_ref[hd] - m_new).astype(BF16)
            out.append((m_new, jnp.exp2(m - m_new) * acc + _dot(vt_ref[0, vj, hd], p)))
        return tuple(out)

    def body(i, carry):
        state, mb = carry
        state = consume(jnp.where(i == 0, qi, i - 1), state, mb)
        return state, produce(i, False)

    init = tuple((jnp.full((1, t), -1e30, F32), jnp.zeros((VT_ROWS, t), F32)) for _ in range(2))
    state, mb = lax.fori_loop(0, qi, body, (init, produce(qi, True)))
    state = consume(jnp.maximum(qi - 1, 0), state, mb)
    ot = [acc[0:FOX_HEAD_DIM, :] / acc[FOX_HEAD_DIM:FOX_HEAD_DIM + 1, :] for _, acc in state]
    o = jnp.concatenate(ot, axis=0).T
    o_ref[0] = (o * g_ref[0]).astype(BF16)


def _attn(q, qx, k, kx, vt, g):
    t = ATT_BLOCK
    nblk = SEQ // t
    qs = pl.BlockSpec((1, t, LANES), lambda b, hp, i: (b, i, hp))
    ks = pl.BlockSpec((1, SEQ, LANES), lambda b, hp, i: (b, 0, hp))
    vts = pl.BlockSpec((1, nblk, 2, VT_ROWS, t), lambda b, hp, i: (b, 0, hp, 0, 0))
    return pl.pallas_call(
        _attn_kernel,
        out_shape=jax.ShapeDtypeStruct((BATCH, SEQ, D_MODEL), BF16),
        grid=(BATCH, FOX_PAIRS, nblk),
        in_specs=[qs, qs, ks, ks, vts, qs],
        out_specs=qs,
        scratch_shapes=[pltpu.VMEM((2, t, t), F32)],
        compiler_params=pltpu.CompilerParams(
            dimension_semantics=("arbitrary", "arbitrary", "arbitrary"), vmem_limit_bytes=VMEM_LIMIT),
        name="fox_attn",
    )(q, qx, k, kx, vt, g)


def _proj_res_kernel(h_ref, a_ref, w_ref, gpost_ref, o_ref):
    y = _dot(a_ref[...], w_ref[...])
    o_ref[...] = h_ref[...] + _rms(y, gpost_ref[...])


def _proj_res(h2d, a2d, w, gpost):
    rows = h2d.shape[0]
    row_spec = pl.BlockSpec((ROW_TILE, D_MODEL), lambda i: (i, 0))
    return pl.pallas_call(
        _proj_res_kernel,
        out_shape=jax.ShapeDtypeStruct((rows, D_MODEL), F32),
        grid=(rows // ROW_TILE,),
        in_specs=[row_spec, row_spec, _resident((D_MODEL, D_MODEL)), _resident((1, D_MODEL))],
        out_specs=row_spec,
        compiler_params=pltpu.CompilerParams(
            dimension_semantics=("arbitrary",), vmem_limit_bytes=VMEM_LIMIT),
        name="fox_out",
    )(h2d, a2d, w, gpost)


def kernel(x, p, ffn1_norm_pre, ffn1_w_in, ffn1_w_out, ffn1_norm_post, mix_norm_pre, mix_norm_post, ffn2_norm_pre, ffn2_w_in, ffn2_w_out, ffn2_norm_post, hgrn_w_in, hgrn_lb_logits, hgrn_out_norm, hgrn_w_out, kv_norm, fox_w_kvf, fox_b_f, fox_w_qg, fox_w_out, ple_norm_pre, ple_w_gate, ple_w_proj, ple_norm_post):
    assert x.shape == (BATCH, SEQ, D_MODEL) and p.shape == (2, BATCH, SEQ, PLE_DIM)
    rows = BATCH * SEQ
    bf = lambda w: w.astype(BF16)
    vec = lambda g: g.reshape(1, -1).astype(F32)
    flat = lambda h: h.reshape(rows, D_MODEL)
    cube = lambda h: h.reshape(BATCH, SEQ, D_MODEL)
    p2 = p.reshape(2 * BATCH, SEQ, PLE_DIM)

    def ffn1(h2d, i):
        return _ffn(h2d, vec(ffn1_norm_pre[i]), bf(ffn1_w_in[i]), bf(ffn1_w_out[i]), vec(ffn1_norm_post[i]))

    def ffn2(h2d, i):
        return _ffn(h2d, vec(ffn2_norm_pre[i]), bf(ffn2_w_in[i]), bf(ffn2_w_out[i]), vec(ffn2_norm_post[i]))

    h = ffn1(flat(x), 0)
    w_heads = bf(hgrn_w_in[0].reshape(D_MODEL, 4, HGRN_HEADS, HGRN_DK).transpose(2, 0, 1, 3)
                 .reshape(HGRN_HEADS, D_MODEL, 4 * HGRN_DK))
    h = _hgrn(cube(h), vec(mix_norm_pre[0]), w_heads, hgrn_lb_logits.astype(F32),
              vec(hgrn_out_norm[0]), bf(hgrn_w_out[0]), vec(mix_norm_post[0]))
    h = ffn2(flat(h), 0)
    pad = LANES - 3 * FOX_HEADS
    wf3 = jnp.pad(jnp.tile(fox_w_kvf[:, 2 * D_MODEL:], (1, 3)), ((0, 0), (0, pad)))
    bf3 = jnp.pad(jnp.tile(fox_b_f, 3), (0, pad)).reshape(1, LANES).astype(F32)
    h, k_sh, kx_sh, qx_sh, vt_sh = _ple_kv(
        cube(h), p2, vec(ple_norm_pre[0]), bf(ple_w_gate[0]), bf(ple_w_proj[0]), vec(ple_norm_post[0]),
        vec(kv_norm), bf(fox_w_kvf[:, :D_MODEL]), bf(fox_w_kvf[:, D_MODEL:2 * D_MODEL].T), bf(wf3), bf3)

    h = ffn1(flat(h), 1)
    q, g = _qg(h, vec(mix_norm_pre[1]), bf(fox_w_qg[0]))
    a = _attn(cube(q), qx_sh, k_sh, kx_sh, vt_sh, cube(g))
    h = _proj_res(h, flat(a), bf(fox_w_out[0]), vec(mix_norm_post[1]))
    h = ffn2(h, 1)
    h = _ple(cube(h), p2, 1, vec(ple_norm_pre[1]), bf(ple_w_gate[1]), bf(ple_w_proj[1]), vec(ple_norm_post[1]))
    return h
```

```python
import functools

import numpy as np
import jax
import jax.numpy as jnp
from jax import lax
from jax.experimental import pallas as pl
from jax.experimental.pallas import tpu as pltpu

D_MODEL = 1024
BATCH = 8
SEQ = 4096
D_FF = 2816
PLE_DIM = 256
NORM_EPS = 1e-6
HGRN_HEADS = 8
HGRN_DK = 128
FOX_HEADS = 16
FOX_HEAD_DIM = 64
FOX_PAIRS = FOX_HEADS // 2

SUBLANES = 8
LANES = 128
MXU_DIM = 256

ROW_TILE = 512
FF_CHUNK = MXU_DIM
HGRN_CHUNK = 128
ATT_BLOCK = 512
BIAS_LANES = 8
VT_ROWS = FOX_HEAD_DIM + 16
LOG2E = 1.4426950408889634
VMEM_LIMIT = 56 * 1024 * 1024

BF16 = jnp.bfloat16
F32 = jnp.float32


def _rms(x, g):
    return x * lax.rsqrt(jnp.mean(x * x, axis=-1, keepdims=True) + NORM_EPS) * g


def _dot(a, b):
    return jnp.dot(a, b, preferred_element_type=F32)


def _dot_nt(a, b):
    return lax.dot_general(a, b, (((1,), (1,)), ((), ())), preferred_element_type=F32)


def _dot_tn(a, b):
    return lax.dot_general(a, b, (((0,), (0,)), ((), ())), preferred_element_type=F32)


def _resident(shape):
    zeros = (0,) * len(shape)
    return pl.BlockSpec(shape, lambda *_: zeros, pipeline_mode=pl.Buffered(1))


def _split3(x):
    h1 = x.astype(BF16)
    r1 = x - h1.astype(F32)
    h2 = r1.astype(BF16)
    h3 = (r1 - h2.astype(F32)).astype(BF16)
    return h1, h2, h3


def _ffn_stage(x, gpre_ref, win_ref, wout_ref, gpost_ref, act_ref):
    xn = _rms(x, gpre_ref[...]).astype(BF16)
    for j in range(D_FF // FF_CHUNK):
        lo = j * FF_CHUNK
        gate = _dot(xn, win_ref[:, lo:lo + FF_CHUNK])
        up = _dot(xn, win_ref[:, D_FF + lo:D_FF + lo + FF_CHUNK])
        act_ref[:, lo:lo + FF_CHUNK] = (gate * jax.nn.sigmoid(gate) * up).astype(BF16)
    y = _dot(act_ref[...], wout_ref[...])
    return x + 0.5 * _rms(y, gpost_ref[...])


def _ffn_specs():
    return [_resident((1, D_MODEL)), _resident((D_MODEL, 2 * D_FF)),
            _resident((D_FF, D_MODEL)), _resident((1, D_MODEL))]


_ACT_SCRATCH = pltpu.VMEM((ROW_TILE, D_FF), BF16)


def _ffn_kernel(x_ref, gpre_ref, win_ref, wout_ref, gpost_ref, o_ref, act_ref):
    o_ref[...] = _ffn_stage(x_ref[...], gpre_ref, win_ref, wout_ref, gpost_ref, act_ref)


def _ffn(h2d, ffn):
    rows = h2d.shape[0]
    row_spec = pl.BlockSpec((ROW_TILE, D_MODEL), lambda i: (i, 0))
    return pl.pallas_call(
        _ffn_kernel,
        out_shape=jax.ShapeDtypeStruct((rows, D_MODEL), F32),
        grid=(rows // ROW_TILE,),
        in_specs=[row_spec] + _ffn_specs(),
        out_specs=row_spec,
        scratch_shapes=[_ACT_SCRATCH],
        compiler_params=pltpu.CompilerParams(
            dimension_semantics=("arbitrary",), vmem_limit_bytes=VMEM_LIMIT),
        name="ffn",
    )(h2d, *ffn)


def _level_table(c):
    t = np.arange(c)[:, None]
    s = np.arange(c)[None, :]
    x = np.maximum(t ^ s, 1)
    lv = np.floor(np.log2(x)).astype(np.int32)
    lv = np.where(s == t, -1, lv)
    lv = np.where(s > t, -2, lv)
    return lv.astype(np.int32)


def _hgrn_chunk(q, z, v, lb, oml, st, lv, tril):
    c = HGRN_CHUNK
    nt = c // SUBLANES
    e = jnp.exp(-jnp.abs(z))
    r = 1.0 / (1.0 + e)
    er = e * r
    pos = z >= 0
    f = lb + oml * jnp.where(pos, r, er)
    kk = oml * jnp.where(pos, er, r)
    lf = jnp.log2(f)

    h1, h2, h3 = _split3(lf)
    cum = _dot(tril, h1) + _dot(tril, h2) + _dot(tril, h3)

    sub = lax.broadcasted_iota(jnp.int32, (SUBLANES, LANES), 0)
    zero = jnp.zeros((SUBLANES, LANES), F32)
    tiles = [cum[SUBLANES * j:SUBLANES * (j + 1), :] for j in range(nt)]

    def row(j, rr):
        return jnp.broadcast_to(cum[SUBLANES * j + rr:SUBLANES * j + rr + 1, :], (SUBLANES, LANES))

    b1 = [row(j, 1) for j in range(nt)]
    b3 = [row(j, 3) for j in range(nt)]
    b5 = [row(j, 5) for j in range(nt)]
    b7 = [row(j, 7) for j in range(nt)]
    qt = [q[SUBLANES * j:SUBLANES * (j + 1), :] for j in range(nt)]
    kt = [kk[SUBLANES * j:SUBLANES * (j + 1), :] for j in range(nt)]

    def cat(ts):
        return jnp.concatenate(ts, axis=0).astype(BF16)

    v_bf = v.astype(BF16)
    kk_bf = kk.astype(BF16)
    sc = jnp.where(lv == -1, _dot_nt(q.astype(BF16), kk_bf), 0.0)
    sc = jnp.where(lv == 0, _dot_nt((q * f).astype(BF16), kk_bf), sc)

    level = 1
    blk = 2
    while blk < c:
        qs, ks = [], []
        for j in range(nt):
            if blk < SUBLANES:
                ref = jnp.where(sub < 4, b1[j], b5[j]) if blk == 2 else b3[j]
                e = jnp.exp2(-jnp.abs(tiles[j] - ref))
                qs.append(qt[j] * e)
                ks.append(kt[j] * e)
            else:
                m = blk // SUBLANES
                b = j // m
                if b % 2 == 1:
                    qs.append(qt[j] * jnp.exp2(tiles[j] - b7[b * m - 1]))
                    ks.append(zero)
                else:
                    qs.append(zero)
                    ks.append(kt[j] * jnp.exp2(b7[b * m + m - 1] - tiles[j]))
        sc = jnp.where(lv == level, _dot_nt(cat(qs), cat(ks)), sc)
        level += 1
        blk *= 2

    last = b7[nt - 1]
    qe = cat([qt[j] * jnp.exp2(tiles[j]) for j in range(nt)])
    ke = cat([kt[j] * jnp.exp2(last - tiles[j]) for j in range(nt)])
    o = _dot_nt(qe, st.astype(BF16)) + _dot(sc.astype(BF16), v_bf)
    st_new = st * jnp.exp2(last[0:1, :]) + _dot_tn(v_bf, ke)
    return o, st_new


def _hgrn_kernel(h_ref, gpre_ref, w_ref, lbl_ref, onorm_ref, wout_ref, gpost_ref, lv_ref, tril_ref,
                 o_ref, xn_ref, p_ref, og_ref, st_ref):
    c = HGRN_CHUNK

    @pl.when(pl.program_id(0) == 0)
    def _():
        st_ref[...] = jnp.zeros_like(st_ref)

    gpre = gpre_ref[...]
    for b in range(BATCH):
        xn_ref[b * c:(b + 1) * c, :] = _rms(h_ref[b], gpre).astype(BF16)

    l0 = lbl_ref[0:1, :]
    l1 = lbl_ref[1:2, :]
    mx = jnp.maximum(l0, l1)
    e0 = jnp.exp(l0 - mx)
    e1 = jnp.exp(l1 - mx)
    lb_all = e0 / (e0 + e1)
    onorm = onorm_ref[...]

    for head in range(HGRN_HEADS):
        p_ref[...] = _dot(xn_ref[...], w_ref[head])
        lb = lb_all[:, head * HGRN_DK:(head + 1) * HGRN_DK]
        oml = 1.0 - lb

        def body(b, carry, head=head, lb=lb, oml=oml):
            r0 = pl.multiple_of(b * c, c)
            blk = p_ref[pl.ds(r0, c), :]
            q = blk[:, 0:HGRN_DK]
            z = blk[:, HGRN_DK:2 * HGRN_DK]
            v = blk[:, 2 * HGRN_DK:3 * HGRN_DK]
            gg = blk[:, 3 * HGRN_DK:4 * HGRN_DK]
            idx = head * BATCH + b
            o, st_new = _hgrn_chunk(q, z, v, lb, oml, st_ref[idx], lv_ref[...], tril_ref[...])
            st_ref[idx] = st_new
            on = _rms(o, onorm)
            og_ref[pl.ds(r0, c), head * HGRN_DK:(head + 1) * HGRN_DK] = (
                on * (gg * jax.nn.sigmoid(gg))).astype(BF16)
            return carry

        lax.fori_loop(0, BATCH, body, 0, unroll=4)

    y = _dot(og_ref[...], wout_ref[...])
    gpost = gpost_ref[...]
    for b in range(BATCH):
        o_ref[b] = h_ref[b] + _rms(y[b * c:(b + 1) * c, :], gpost)


def _hgrn(h, gpre, w_heads, lb_logits, onorm, wout, gpost):
    c = HGRN_CHUNK
    lv = jnp.asarray(_level_table(c))
    tril = jnp.asarray(np.tril(np.ones((c, c), np.float32)), dtype=BF16)
    blk = pl.BlockSpec((BATCH, c, D_MODEL), lambda j: (0, j, 0))
    return pl.pallas_call(
        _hgrn_kernel,
        out_shape=jax.ShapeDtypeStruct((BATCH, SEQ, D_MODEL), F32),
        grid=(SEQ // c,),
        in_specs=[blk, _resident((1, D_MODEL)), _resident((HGRN_HEADS, D_MODEL, 4 * HGRN_DK)),
                  _resident((2, D_MODEL)), _resident((1, HGRN_DK)), _resident((D_MODEL, D_MODEL)),
                  _resident((1, D_MODEL)), _resident((c, c)), _resident((c, c))],
        out_specs=blk,
        scratch_shapes=[pltpu.VMEM((BATCH * c, D_MODEL), BF16),
                        pltpu.VMEM((BATCH * c, 4 * HGRN_DK), F32),
                        pltpu.VMEM((BATCH * c, D_MODEL), BF16),
                        pltpu.VMEM((HGRN_HEADS * BATCH, HGRN_DK, HGRN_DK), F32)],
        compiler_params=pltpu.CompilerParams(
            dimension_semantics=("arbitrary",), vmem_limit_bytes=VMEM_LIMIT),
        name="hgrn",
    )(h, gpre, w_heads, lb_logits, onorm, wout, gpost, lv, tril)


def _ple_stage(x, p, gpre_ref, wg_ref, wp_ref, gpost_ref):
    xn = _rms(x, gpre_ref[...]).astype(BF16)
    gate = jax.nn.sigmoid(_dot(xn, wg_ref[...]))
    pp = _dot(p.astype(BF16), wp_ref[...])
    return x + _rms(gate * pp, gpost_ref[...])


def _ple_specs():
    return [_resident((1, D_MODEL)), _resident((D_MODEL, D_MODEL)),
            _resident((PLE_DIM, D_MODEL)), _resident((1, D_MODEL))]


def _ffn_ple_kv_kernel(h_ref, p_ref, f_gpre, f_win, f_wout, f_gpost, gpre_ref, wg_ref, wp_ref, gpost_ref,
                       kvn_ref, wk_ref, wvt_ref, wf_ref, bf_ref, tril_ref, ek_ref, eq_ref,
                       o_ref, k_ref, kx_ref, qx_ref, vt_ref, act_ref, carry_ref):
    hn = _ffn_stage(h_ref[0], f_gpre, f_win, f_wout, f_gpost, act_ref)
    hn = _ple_stage(hn, p_ref[0], gpre_ref, wg_ref, wp_ref, gpost_ref)
    o_ref[0] = hn
    xn = _rms(hn, kvn_ref[...]).astype(BF16)
    k_ref[0] = _dot(xn, wk_ref[...]).astype(BF16)
    vt = _dot_nt(wvt_ref[...], xn).astype(BF16)
    ones = jnp.ones((VT_ROWS - FOX_HEAD_DIM, ROW_TILE), BF16)
    for hd in range(FOX_HEADS):
        vt_ref[0, 0, hd, 0:FOX_HEAD_DIM, :] = vt[hd * FOX_HEAD_DIM:(hd + 1) * FOX_HEAD_DIM, :]
        vt_ref[0, 0, hd, FOX_HEAD_DIM:VT_ROWS, :] = ones

    @pl.when(pl.program_id(1) == 0)
    def _():
        carry_ref[...] = jnp.zeros_like(carry_ref)

    x = _dot(xn, wf_ref[...]) + bf_ref[...]
    ls = jnp.minimum(x, 0.0) - jnp.log1p(jnp.exp(-jnp.abs(x)))
    h1, h2, h3 = _split3(ls)
    tril = tril_ref[...]
    cum = _dot(tril, h1) + _dot(tril, h2) + _dot(tril, h3) + carry_ref[0:1, :]
    carry_ref[...] = jnp.broadcast_to(cum[ROW_TILE - 1:ROW_TILE, :], carry_ref.shape)
    c2 = cum * LOG2E
    hi = c2.astype(BF16).astype(F32)
    r1 = c2 - hi
    mid = r1.astype(BF16).astype(F32)
    lo = r1 - mid
    lane = lax.broadcasted_iota(jnp.int32, c2.shape, 1)
    cx = jnp.where(lane < FOX_HEADS, hi, jnp.where(lane < 2 * FOX_HEADS, mid, jnp.where(
        lane < 3 * FOX_HEADS, lo, jnp.where(lane == 3 * FOX_HEADS, 1.0, 0.0)))).astype(BF16)
    kx_ref[0] = _dot(cx, ek_ref[...]).astype(BF16)
    qx_ref[0] = _dot(cx, eq_ref[...]).astype(BF16)


def _bias_placement():
    ek = np.zeros((LANES, D_MODEL), np.float32)
    eq = np.zeros((LANES, D_MODEL), np.float32)
    one_lane = 3 * FOX_HEADS
    for head in range(FOX_HEADS):
        base = (head // 2) * LANES + (head % 2) * BIAS_LANES
        for i in range(3):
            ek[i * FOX_HEADS + head, base + i] = 1.0
            eq[one_lane, base + i] = -1.0
            ek[one_lane, base + 3 + i] = 1.0
            eq[i * FOX_HEADS + head, base + 3 + i] = 1.0
    return jnp.asarray(ek, dtype=BF16), jnp.asarray(eq, dtype=BF16)


def _ffn_ple_kv(h, p, ffn, ple, kvn, wk, wvt, wf3, bf3):
    assert ROW_TILE == ATT_BLOCK
    nblk = SEQ // ROW_TILE
    hs = pl.BlockSpec((1, ROW_TILE, D_MODEL), lambda b, i: (b, i, 0))
    ps = pl.BlockSpec((1, ROW_TILE, PLE_DIM), lambda b, i: (b, i, 0))
    vts = pl.BlockSpec((1, 1, FOX_HEADS, VT_ROWS, ROW_TILE), lambda b, i: (b, i, 0, 0, 0))
    tril = jnp.asarray(np.tril(np.ones((ROW_TILE, ROW_TILE), np.float32)), dtype=BF16)
    ek, eq = _bias_placement()
    return pl.pallas_call(
        _ffn_ple_kv_kernel,
        out_shape=(jax.ShapeDtypeStruct((BATCH, SEQ, D_MODEL), F32),
                   jax.ShapeDtypeStruct((BATCH, SEQ, D_MODEL), BF16),
                   jax.ShapeDtypeStruct((BATCH, SEQ, D_MODEL), BF16),
                   jax.ShapeDtypeStruct((BATCH, SEQ, D_MODEL), BF16),
                   jax.ShapeDtypeStruct((BATCH, nblk, FOX_HEADS, VT_ROWS, ROW_TILE), BF16)),
        grid=(BATCH, nblk),
        in_specs=[hs, ps] + _ffn_specs() + _ple_specs() + [
            _resident((1, D_MODEL)), _resident((D_MODEL, D_MODEL)), _resident((D_MODEL, D_MODEL)),
            _resident((D_MODEL, LANES)), _resident((1, LANES)),
            _resident((ROW_TILE, ROW_TILE)), _resident((LANES, D_MODEL)), _resident((LANES, D_MODEL))],
        out_specs=(hs, hs, hs, hs, vts),
        scratch_shapes=[_ACT_SCRATCH, pltpu.VMEM((SUBLANES, LANES), F32)],
        compiler_params=pltpu.CompilerParams(
            dimension_semantics=("arbitrary", "arbitrary"), vmem_limit_bytes=VMEM_LIMIT),
        name="ffn_ple_kv",
    )(h, p, *ffn, *ple, kvn, wk, wvt, wf3, bf3, tril, ek, eq)


def _ffn_qg_kernel(h_ref, f_gpre, f_win, f_wout, f_gpost, gpre_ref, w_ref, o_ref, q_ref, g_ref, act_ref):
    hn = _ffn_stage(h_ref[...], f_gpre, f_win, f_wout, f_gpost, act_ref)
    o_ref[...] = hn
    xn = _rms(hn, gpre_ref[...]).astype(BF16)
    qg = _dot(xn, w_ref[...])
    q_ref[...] = (qg[:, :D_MODEL] * (FOX_HEAD_DIM ** -0.5 * LOG2E)).astype(BF16)
    g_ref[...] = jax.nn.sigmoid(qg[:, D_MODEL:])


def _ffn_qg(h2d, ffn, gpre, w):
    rows = h2d.shape[0]
    row_spec = pl.BlockSpec((ROW_TILE, D_MODEL), lambda i: (i, 0))
    return pl.pallas_call(
        _ffn_qg_kernel,
        out_shape=(jax.ShapeDtypeStruct((rows, D_MODEL), F32),
                   jax.ShapeDtypeStruct((rows, D_MODEL), BF16),
                   jax.ShapeDtypeStruct((rows, D_MODEL), F32)),
        grid=(rows // ROW_TILE,),
        in_specs=[row_spec] + _ffn_specs() + [_resident((1, D_MODEL)), _resident((D_MODEL, 2 * D_MODEL))],
        out_specs=(row_spec, row_spec, row_spec),
        scratch_shapes=[_ACT_SCRATCH],
        compiler_params=pltpu.CompilerParams(
            dimension_semantics=("arbitrary",), vmem_limit_bytes=VMEM_LIMIT),
        name="ffn_qg",
    )(h2d, *ffn, gpre, w)


def _attn_schedule():
    t = ATT_BLOCK
    items = []
    for qi in range(SEQ // t):
        if qi % 2 == 0:
            blk = [(qi * t, t, 0)]
        else:
            blk = [((qi - 1) * t, 2 * t, t)]
        blk += [(2 * j * t, 2 * t, None) for j in range(qi // 2)]
        for n, (k0, nk, off) in enumerate(blk):
            items.append((qi, k0, nk, off, n == 0, n == len(blk) - 1))
    return items


def _attn_kernel(q_ref, qx_ref, k_ref, kx_ref, vt_ref, g_ref, o_ref, s0_ref, s1_ref, qcat_ref):
    t = ATT_BLOCK
    s_refs = (s0_ref, s1_ref)
    lane = lax.broadcasted_iota(jnp.int32, (t, LANES), 1)

    def load_queries(qi):
        q2 = q_ref[0, qi * t:(qi + 1) * t, :].astype(F32)
        qx2 = qx_ref[0, qi * t:(qi + 1) * t, :].astype(F32)
        for hd in range(2):
            in_head = (lane >= hd * FOX_HEAD_DIM) & (lane < (hd + 1) * FOX_HEAD_DIM)
            in_bias = (lane >= hd * BIAS_LANES) & (lane < (hd + 1) * BIAS_LANES)
            qcat_ref[hd, :, 0:LANES] = jnp.where(in_head, q2, 0.0).astype(BF16)
            qcat_ref[hd, :, LANES:2 * LANES] = jnp.where(in_bias, qx2, 0.0).astype(BF16)

    def produce(hd, item):
        _, k0, nk, off, _, _ = item
        kcat = jnp.concatenate([k_ref[0, k0:k0 + nk, :], kx_ref[0, k0:k0 + nk, :]], axis=1)
        st = _dot_nt(kcat, qcat_ref[hd])
        if off is not None:
            key = lax.broadcasted_iota(jnp.int32, (nk, t), 0)
            qry = lax.broadcasted_iota(jnp.int32, (nk, t), 1)
            st = jnp.where(key <= qry + off, st, -1e30)
        s_refs[hd][0:nk, :] = st
        return jnp.max(st, axis=0, keepdims=True)

    def consume(hd, item, state, mb):
        _, k0, nk, _, _, _ = item
        m, acc = state
        m_new = jnp.maximum(m, mb)
        p = jnp.exp2(s_refs[hd][0:nk, :] - m_new).astype(BF16)
        vt = jnp.concatenate([vt_ref[0, kb, hd] for kb in range(k0 // t, (k0 + nk) // t)], axis=1)
        return m_new, jnp.exp2(m - m_new) * acc + _dot(vt, p)

    def finalize(qi, states):
        ot = [acc[0:FOX_HEAD_DIM, :] / acc[FOX_HEAD_DIM:FOX_HEAD_DIM + 1, :] for _, acc in states]
        o = jnp.concatenate(ot, axis=0).T
        o_ref[0, qi * t:(qi + 1) * t, :] = (o * g_ref[0, qi * t:(qi + 1) * t, :]).astype(BF16)

    init = (jnp.full((1, t), -1e30, F32), jnp.zeros((VT_ROWS, t), F32))

    items = _attn_schedule()
    st = [init, init]
    mb = [None, None]
    done0 = None
    for g, item in enumerate(items):
        qi, first = item[0], item[4]
        if first:
            load_queries(qi)
        mb[0] = produce(0, item)
        if g > 0:
            prev = items[g - 1]
            st[1] = consume(1, prev, st[1], mb[1])
            if prev[5]:
                finalize(prev[0], (done0, st[1]))
                st[1] = init
        st[0] = consume(0, item, init if first else st[0], mb[0])
        if item[5]:
            done0 = st[0]
        mb[1] = produce(1, item)
    last = items[-1]
    finalize(last[0], (done0, consume(1, last, st[1], mb[1])))


def _attn(q, qx, k, kx, vt, g):
    t = ATT_BLOCK
    nblk = SEQ // t
    ks = pl.BlockSpec((1, SEQ, LANES), lambda b, hp: (b, 0, hp))
    vts = pl.BlockSpec((1, nblk, 2, VT_ROWS, t), lambda b, hp: (b, 0, hp, 0, 0))
    return pl.pallas_call(
        _attn_kernel,
        out_shape=jax.ShapeDtypeStruct((BATCH, SEQ, D_MODEL), BF16),
        grid=(BATCH, FOX_PAIRS),
        in_specs=[ks, ks, ks, ks, vts, ks],
        out_specs=ks,
        scratch_shapes=[pltpu.VMEM((2 * t, t), F32), pltpu.VMEM((2 * t, t), F32),
                        pltpu.VMEM((2, t, 2 * LANES), BF16)],
        compiler_params=pltpu.CompilerParams(
            dimension_semantics=("arbitrary", "arbitrary"), vmem_limit_bytes=VMEM_LIMIT),
        name="fox_attn",
    )(q, qx, k, kx, vt, g)


def _out_ffn_ple_kernel(h_ref, a_ref, p_ref, w_ref, mpost_ref, f_gpre, f_win, f_wout, f_gpost,
                        gpre_ref, wg_ref, wp_ref, gpost_ref, o_ref, act_ref):
    hn = h_ref[...] + _rms(_dot(a_ref[...], w_ref[...]), mpost_ref[...])
    hn = _ffn_stage(hn, f_gpre, f_win, f_wout, f_gpost, act_ref)
    o_ref[...] = _ple_stage(hn, p_ref[...], gpre_ref, wg_ref, wp_ref, gpost_ref)


def _out_ffn_ple(h2d, a2d, p2d, p_row0, w, mpost, ffn, ple):
    rows = h2d.shape[0]
    row_spec = pl.BlockSpec((ROW_TILE, D_MODEL), lambda i: (i, 0))
    p_spec = pl.BlockSpec((ROW_TILE, PLE_DIM), lambda i: (p_row0 // ROW_TILE + i, 0))
    return pl.pallas_call(
        _out_ffn_ple_kernel,
        out_shape=jax.ShapeDtypeStruct((rows, D_MODEL), F32),
        grid=(rows // ROW_TILE,),
        in_specs=[row_spec, row_spec, p_spec, _resident((D_MODEL, D_MODEL)), _resident((1, D_MODEL))]
        + _ffn_specs() + _ple_specs(),
        out_specs=row_spec,
        scratch_shapes=[_ACT_SCRATCH],
        compiler_params=pltpu.CompilerParams(
            dimension_semantics=("arbitrary",), vmem_limit_bytes=VMEM_LIMIT),
        name="out_ffn_ple",
    )(h2d, a2d, p2d, w, mpost, *ffn, *ple)


def kernel(x, p, ffn1_norm_pre, ffn1_w_in, ffn1_w_out, ffn1_norm_post, mix_norm_pre, mix_norm_post, ffn2_norm_pre, ffn2_w_in, ffn2_w_out, ffn2_norm_post, hgrn_w_in, hgrn_lb_logits, hgrn_out_norm, hgrn_w_out, kv_norm, fox_w_kvf, fox_b_f, fox_w_qg, fox_w_out, ple_norm_pre, ple_w_gate, ple_w_proj, ple_norm_post):
    assert x.shape == (BATCH, SEQ, D_MODEL) and p.shape == (2, BATCH, SEQ, PLE_DIM)
    rows = BATCH * SEQ
    bf = lambda w: w.astype(BF16)
    vec = lambda g: g.reshape(1, -1).astype(F32)
    flat = lambda h: h.reshape(rows, D_MODEL)
    cube = lambda h: h.reshape(BATCH, SEQ, D_MODEL)
    p2 = p.reshape(2 * BATCH, SEQ, PLE_DIM)

    def ffn1(i):
        return vec(ffn1_norm_pre[i]), bf(ffn1_w_in[i]), bf(ffn1_w_out[i]), vec(ffn1_norm_post[i])

    def ffn2(i):
        return vec(ffn2_norm_pre[i]), bf(ffn2_w_in[i]), bf(ffn2_w_out[i]), vec(ffn2_norm_post[i])

    def ple(i):
        return vec(ple_norm_pre[i]), bf(ple_w_gate[i]), bf(ple_w_proj[i]), vec(ple_norm_post[i])

    h = _ffn(flat(x), ffn1(0))
    w_heads = bf(hgrn_w_in[0].reshape(D_MODEL, 4, HGRN_HEADS, HGRN_DK).transpose(2, 0, 1, 3)
                 .reshape(HGRN_HEADS, D_MODEL, 4 * HGRN_DK))
    h = _hgrn(cube(h), vec(mix_norm_pre[0]), w_heads, hgrn_lb_logits.astype(F32),
              vec(hgrn_out_norm[0]), bf(hgrn_w_out[0]), vec(mix_norm_post[0]))
    pad = LANES - 3 * FOX_HEADS
    wf3 = jnp.pad(jnp.tile(fox_w_kvf[:, 2 * D_MODEL:], (1, 3)), ((0, 0), (0, pad)))
    bf3 = jnp.pad(jnp.tile(fox_b_f, 3), (0, pad)).reshape(1, LANES).astype(F32)
    h, k_sh, kx_sh, qx_sh, vt_sh = _ffn_ple_kv(
        h, p2, ffn2(0), ple(0),
        vec(kv_norm), bf(fox_w_kvf[:, :D_MODEL]), bf(fox_w_kvf[:, D_MODEL:2 * D_MODEL].T), bf(wf3), bf3)

    h, q, g = _ffn_qg(flat(h), ffn1(1), vec(mix_norm_pre[1]), bf(fox_w_qg[0]))
    a = _attn(cube(q), qx_sh, k_sh, kx_sh, vt_sh, cube(g))
    h = _out_ffn_ple(h, flat(a), p2.reshape(2 * rows, PLE_DIM), rows, bf(fox_w_out[0]),
                     vec(mix_norm_post[1]), ffn2(1), ple(1))
    return cube(h)
```

```python
import functools

import numpy as np
import jax
import jax.numpy as jnp
from jax import lax
from jax.experimental import pallas as pl
from jax.experimental.pallas import tpu as pltpu

D_MODEL = 1024
BATCH = 8
SEQ = 4096
D_FF = 2816
PLE_DIM = 256
NORM_EPS = 1e-6
HGRN_HEADS = 8
HGRN_DK = 128
FOX_HEADS = 16
FOX_HEAD_DIM = 64
FOX_PAIRS = FOX_HEADS // 2

SUBLANES = 8
LANES = 128
MXU_DIM = 256

ROW_TILE = 512
FF_CHUNK = MXU_DIM
HGRN_CHUNK = 128
ATT_BLOCK = 512
ATT_QCHUNK = MXU_DIM
BIAS_LANES = 8
VT_ROWS = FOX_HEAD_DIM + 16
LOG2E = 1.4426950408889634
VMEM_LIMIT = 56 * 1024 * 1024

BF16 = jnp.bfloat16
F32 = jnp.float32


def _rms(x, g):
    return x * lax.rsqrt(jnp.mean(x * x, axis=-1, keepdims=True) + NORM_EPS) * g


def _dot(a, b):
    return jnp.dot(a, b, preferred_element_type=F32)


def _dot_nt(a, b):
    return lax.dot_general(a, b, (((1,), (1,)), ((), ())), preferred_element_type=F32)


def _dot_tn(a, b):
    return lax.dot_general(a, b, (((0,), (0,)), ((), ())), preferred_element_type=F32)


def _resident(shape):
    zeros = (0,) * len(shape)
    return pl.BlockSpec(shape, lambda *_: zeros, pipeline_mode=pl.Buffered(1))


def _split3(x):
    h1 = x.astype(BF16)
    r1 = x - h1.astype(F32)
    h2 = r1.astype(BF16)
    h3 = (r1 - h2.astype(F32)).astype(BF16)
    return h1, h2, h3


def _ffn_stage(x, gpre_ref, win_ref, wout_ref, gpost_ref, act_ref):
    xn = _rms(x, gpre_ref[...]).astype(BF16)
    for j in range(D_FF // FF_CHUNK):
        lo = j * FF_CHUNK
        gate = _dot(xn, win_ref[:, lo:lo + FF_CHUNK])
        up = _dot(xn, win_ref[:, D_FF + lo:D_FF + lo + FF_CHUNK])
        act_ref[:, lo:lo + FF_CHUNK] = (gate * jax.nn.sigmoid(gate) * up).astype(BF16)
    y = _dot(act_ref[...], wout_ref[...])
    return x + 0.5 * _rms(y, gpost_ref[...])


def _ffn_specs():
    return [_resident((1, D_MODEL)), _resident((D_MODEL, 2 * D_FF)),
            _resident((D_FF, D_MODEL)), _resident((1, D_MODEL))]


_ACT_SCRATCH = pltpu.VMEM((ROW_TILE, D_FF), BF16)


def _ffn_kernel(x_ref, gpre_ref, win_ref, wout_ref, gpost_ref, o_ref, act_ref):
    o_ref[...] = _ffn_stage(x_ref[...], gpre_ref, win_ref, wout_ref, gpost_ref, act_ref)


def _ffn(h2d, ffn):
    rows = h2d.shape[0]
    row_spec = pl.BlockSpec((ROW_TILE, D_MODEL), lambda i: (i, 0))
    return pl.pallas_call(
        _ffn_kernel,
        out_shape=jax.ShapeDtypeStruct((rows, D_MODEL), F32),
        grid=(rows // ROW_TILE,),
        in_specs=[row_spec] + _ffn_specs(),
        out_specs=row_spec,
        scratch_shapes=[_ACT_SCRATCH],
        compiler_params=pltpu.CompilerParams(
            dimension_semantics=("arbitrary",), vmem_limit_bytes=VMEM_LIMIT),
        name="ffn",
    )(h2d, *ffn)


def _level_table(c):
    t = np.arange(c)[:, None]
    s = np.arange(c)[None, :]
    x = np.maximum(t ^ s, 1)
    lv = np.floor(np.log2(x)).astype(np.int32)
    lv = np.where(s == t, -1, lv)
    lv = np.where(s > t, -2, lv)
    return lv.astype(np.int32)


def _hgrn_chunk(q, z, v, lb, oml, st, lv, tril):
    c = HGRN_CHUNK
    nt = c // SUBLANES
    e = jnp.exp(-jnp.abs(z))
    r = 1.0 / (1.0 + e)
    er = e * r
    pos = z >= 0
    f = lb + oml * jnp.where(pos, r, er)
    kk = oml * jnp.where(pos, er, r)
    lf = jnp.log2(f)

    h1, h2, h3 = _split3(lf)
    cum = _dot(tril, h1) + _dot(tril, h2) + _dot(tril, h3)

    sub = lax.broadcasted_iota(jnp.int32, (SUBLANES, LANES), 0)
    zero = jnp.zeros((SUBLANES, LANES), F32)
    tiles = [cum[SUBLANES * j:SUBLANES * (j + 1), :] for j in range(nt)]

    def row(j, rr):
        return jnp.broadcast_to(cum[SUBLANES * j + rr:SUBLANES * j + rr + 1, :], (SUBLANES, LANES))

    b1 = [row(j, 1) for j in range(nt)]
    b3 = [row(j, 3) for j in range(nt)]
    b5 = [row(j, 5) for j in range(nt)]
    b7 = [row(j, 7) for j in range(nt)]
    qt = [q[SUBLANES * j:SUBLANES * (j + 1), :] for j in range(nt)]
    kt = [kk[SUBLANES * j:SUBLANES * (j + 1), :] for j in range(nt)]

    def cat(ts):
        return jnp.concatenate(ts, axis=0).astype(BF16)

    v_bf = v.astype(BF16)
    kk_bf = kk.astype(BF16)
    sc = jnp.where(lv == -1, _dot_nt(q.astype(BF16), kk_bf), 0.0)
    sc = jnp.where(lv == 0, _dot_nt((q * f).astype(BF16), kk_bf), sc)

    level = 1
    blk = 2
    while blk < c:
        qs, ks = [], []
        for j in range(nt):
            if blk < SUBLANES:
                ref = jnp.where(sub < 4, b1[j], b5[j]) if blk == 2 else b3[j]
                e = jnp.exp2(-jnp.abs(tiles[j] - ref))
                qs.append(qt[j] * e)
                ks.append(kt[j] * e)
            else:
                m = blk // SUBLANES
                b = j // m
                if b % 2 == 1:
                    qs.append(qt[j] * jnp.exp2(tiles[j] - b7[b * m - 1]))
                    ks.append(zero)
                else:
                    qs.append(zero)
                    ks.append(kt[j] * jnp.exp2(b7[b * m + m - 1] - tiles[j]))
        sc = jnp.where(lv == level, _dot_nt(cat(qs), cat(ks)), sc)
        level += 1
        blk *= 2

    last = b7[nt - 1]
    qe = cat([qt[j] * jnp.exp2(tiles[j]) for j in range(nt)])
    ke = cat([kt[j] * jnp.exp2(last - tiles[j]) for j in range(nt)])
    o = _dot_nt(qe, st.astype(BF16)) + _dot(sc.astype(BF16), v_bf)
    st_new = st * jnp.exp2(last[0:1, :]) + _dot_tn(v_bf, ke)
    return o, st_new


def _hgrn_kernel(h_ref, gpre_ref, w_ref, lbl_ref, onorm_ref, wout_ref, gpost_ref, lv_ref, tril_ref,
                 o_ref, xn_ref, p_ref, og_ref, st_ref):
    c = HGRN_CHUNK

    @pl.when(pl.program_id(0) == 0)
    def _():
        st_ref[...] = jnp.zeros_like(st_ref)

    gpre = gpre_ref[...]
    for b in range(BATCH):
        xn_ref[b * c:(b + 1) * c, :] = _rms(h_ref[b], gpre).astype(BF16)

    l0 = lbl_ref[0:1, :]
    l1 = lbl_ref[1:2, :]
    mx = jnp.maximum(l0, l1)
    e0 = jnp.exp(l0 - mx)
    e1 = jnp.exp(l1 - mx)
    lb_all = e0 / (e0 + e1)
    onorm = onorm_ref[...]

    for head in range(HGRN_HEADS):
        p_ref[...] = _dot(xn_ref[...], w_ref[head])
        lb = lb_all[:, head * HGRN_DK:(head + 1) * HGRN_DK]
        oml = 1.0 - lb
        for b in range(BATCH):
            blk = p_ref[b * c:(b + 1) * c, :]
            q = blk[:, 0:HGRN_DK]
            z = blk[:, HGRN_DK:2 * HGRN_DK]
            v = blk[:, 2 * HGRN_DK:3 * HGRN_DK]
            gg = blk[:, 3 * HGRN_DK:4 * HGRN_DK]
            idx = head * BATCH + b
            o, st_new = _hgrn_chunk(q, z, v, lb, oml, st_ref[idx], lv_ref[...], tril_ref[...])
            st_ref[idx] = st_new
            on = _rms(o, onorm)
            og_ref[b * c:(b + 1) * c, head * HGRN_DK:(head + 1) * HGRN_DK] = (
                on * (gg * jax.nn.sigmoid(gg))).astype(BF16)

    y = _dot(og_ref[...], wout_ref[...])
    gpost = gpost_ref[...]
    for b in range(BATCH):
        o_ref[b] = h_ref[b] + _rms(y[b * c:(b + 1) * c, :], gpost)


def _hgrn(h, gpre, w_heads, lb_logits, onorm, wout, gpost):
    c = HGRN_CHUNK
    lv = jnp.asarray(_level_table(c))
    tril = jnp.asarray(np.tril(np.ones((c, c), np.float32)), dtype=BF16)
    blk = pl.BlockSpec((BATCH, c, D_MODEL), lambda j: (0, j, 0))
    return pl.pallas_call(
        _hgrn_kernel,
        out_shape=jax.ShapeDtypeStruct((BATCH, SEQ, D_MODEL), F32),
        grid=(SEQ // c,),
        in_specs=[blk, _resident((1, D_MODEL)), _resident((HGRN_HEADS, D_MODEL, 4 * HGRN_DK)),
                  _resident((2, D_MODEL)), _resident((1, HGRN_DK)), _resident((D_MODEL, D_MODEL)),
                  _resident((1, D_MODEL)), _resident((c, c)), _resident((c, c))],
        out_specs=blk,
        scratch_shapes=[pltpu.VMEM((BATCH * c, D_MODEL), BF16),
                        pltpu.VMEM((BATCH * c, 4 * HGRN_DK), F32),
                        pltpu.VMEM((BATCH * c, D_MODEL), BF16),
                        pltpu.VMEM((HGRN_HEADS * BATCH, HGRN_DK, HGRN_DK), F32)],
        compiler_params=pltpu.CompilerParams(
            dimension_semantics=("arbitrary",), vmem_limit_bytes=VMEM_LIMIT),
        name="hgrn",
    )(h, gpre, w_heads, lb_logits, onorm, wout, gpost, lv, tril)


def _ple_stage(x, p, gpre_ref, wg_ref, wp_ref, gpost_ref):
    xn = _rms(x, gpre_ref[...]).astype(BF16)
    gate = jax.nn.sigmoid(_dot(xn, wg_ref[...]))
    pp = _dot(p.astype(BF16), wp_ref[...])
    return x + _rms(gate * pp, gpost_ref[...])


def _ple_specs():
    return [_resident((1, D_MODEL)), _resident((D_MODEL, D_MODEL)),
            _resident((PLE_DIM, D_MODEL)), _resident((1, D_MODEL))]


def _ffn_ple_kv_kernel(h_ref, p_ref, f_gpre, f_win, f_wout, f_gpost, gpre_ref, wg_ref, wp_ref, gpost_ref,
                       kvn_ref, wk_ref, wvt_ref, wf_ref, bf_ref, tril_ref, ek_ref, eq_ref,
                       o_ref, k_ref, kx_ref, qx_ref, vt_ref, act_ref, carry_ref):
    hn = _ffn_stage(h_ref[0], f_gpre, f_win, f_wout, f_gpost, act_ref)
    hn = _ple_stage(hn, p_ref[0], gpre_ref, wg_ref, wp_ref, gpost_ref)
    o_ref[0] = hn
    xn = _rms(hn, kvn_ref[...]).astype(BF16)
    k_ref[0] = _dot(xn, wk_ref[...]).astype(BF16)
    vt = _dot_nt(wvt_ref[...], xn).astype(BF16)
    ones = jnp.ones((VT_ROWS - FOX_HEAD_DIM, ROW_TILE), BF16)
    for hd in range(FOX_HEADS):
        vt_ref[0, 0, hd, 0:FOX_HEAD_DIM, :] = vt[hd * FOX_HEAD_DIM:(hd + 1) * FOX_HEAD_DIM, :]
        vt_ref[0, 0, hd, FOX_HEAD_DIM:VT_ROWS, :] = ones

    @pl.when(pl.program_id(1) == 0)
    def _():
        carry_ref[...] = jnp.zeros_like(carry_ref)

    x = _dot(xn, wf_ref[...]) + bf_ref[...]
    ls = jnp.minimum(x, 0.0) - jnp.log1p(jnp.exp(-jnp.abs(x)))
    h1, h2, h3 = _split3(ls)
    tril = tril_ref[...]
    cum = _dot(tril, h1) + _dot(tril, h2) + _dot(tril, h3) + carry_ref[0:1, :]
    carry_ref[...] = jnp.broadcast_to(cum[ROW_TILE - 1:ROW_TILE, :], carry_ref.shape)
    c2 = cum * LOG2E
    hi = c2.astype(BF16).astype(F32)
    r1 = c2 - hi
    mid = r1.astype(BF16).astype(F32)
    lo = r1 - mid
    lane = lax.broadcasted_iota(jnp.int32, c2.shape, 1)
    cx = jnp.where(lane < FOX_HEADS, hi, jnp.where(lane < 2 * FOX_HEADS, mid, jnp.where(
        lane < 3 * FOX_HEADS, lo, jnp.where(lane == 3 * FOX_HEADS, 1.0, 0.0)))).astype(BF16)
    kx_ref[0] = _dot(cx, ek_ref[...]).astype(BF16)
    qx_ref[0] = _dot(cx, eq_ref[...]).astype(BF16)


def _bias_placement():
    ek = np.zeros((LANES, D_MODEL), np.float32)
    eq = np.zeros((LANES, D_MODEL), np.float32)
    one_lane = 3 * FOX_HEADS
    for head in range(FOX_HEADS):
        base = (head // 2) * LANES + (head % 2) * BIAS_LANES
        for i in range(3):
            ek[i * FOX_HEADS + head, base + i] = 1.0
            eq[one_lane, base + i] = -1.0
            ek[one_lane, base + 3 + i] = 1.0
            eq[i * FOX_HEADS + head, base + 3 + i] = 1.0
    return jnp.asarray(ek, dtype=BF16), jnp.asarray(eq, dtype=BF16)


def _ffn_ple_kv(h, p, ffn, ple, kvn, wk, wvt, wf3, bf3):
    assert ROW_TILE == ATT_BLOCK
    nblk = SEQ // ROW_TILE
    hs = pl.BlockSpec((1, ROW_TILE, D_MODEL), lambda b, i: (b, i, 0))
    ps = pl.BlockSpec((1, ROW_TILE, PLE_DIM), lambda b, i: (b, i, 0))
    vts = pl.BlockSpec((1, 1, FOX_HEADS, VT_ROWS, ROW_TILE), lambda b, i: (b, i, 0, 0, 0))
    tril = jnp.asarray(np.tril(np.ones((ROW_TILE, ROW_TILE), np.float32)), dtype=BF16)
    ek, eq = _bias_placement()
    return pl.pallas_call(
        _ffn_ple_kv_kernel,
        out_shape=(jax.ShapeDtypeStruct((BATCH, SEQ, D_MODEL), F32),
                   jax.ShapeDtypeStruct((BATCH, SEQ, D_MODEL), BF16),
                   jax.ShapeDtypeStruct((BATCH, SEQ, D_MODEL), BF16),
                   jax.ShapeDtypeStruct((BATCH, SEQ, D_MODEL), BF16),
                   jax.ShapeDtypeStruct((BATCH, nblk, FOX_HEADS, VT_ROWS, ROW_TILE), BF16)),
        grid=(BATCH, nblk),
        in_specs=[hs, ps] + _ffn_specs() + _ple_specs() + [
            _resident((1, D_MODEL)), _resident((D_MODEL, D_MODEL)), _resident((D_MODEL, D_MODEL)),
            _resident((D_MODEL, LANES)), _resident((1, LANES)),
            _resident((ROW_TILE, ROW_TILE)), _resident((LANES, D_MODEL)), _resident((LANES, D_MODEL))],
        out_specs=(hs, hs, hs, hs, vts),
        scratch_shapes=[_ACT_SCRATCH, pltpu.VMEM((SUBLANES, LANES), F32)],
        compiler_params=pltpu.CompilerParams(
            dimension_semantics=("arbitrary", "arbitrary"), vmem_limit_bytes=VMEM_LIMIT),
        name="ffn_ple_kv",
    )(h, p, *ffn, *ple, kvn, wk, wvt, wf3, bf3, tril, ek, eq)


def _ffn_qg_kernel(h_ref, f_gpre, f_win, f_wout, f_gpost, gpre_ref, w_ref, o_ref, q_ref, g_ref, act_ref):
    hn = _ffn_stage(h_ref[...], f_gpre, f_win, f_wout, f_gpost, act_ref)
    o_ref[...] = hn
    xn = _rms(hn, gpre_ref[...]).astype(BF16)
    qg = _dot(xn, w_ref[...])
    q_ref[...] = (qg[:, :D_MODEL] * (FOX_HEAD_DIM ** -0.5 * LOG2E)).astype(BF16)
    g_ref[...] = jax.nn.sigmoid(qg[:, D_MODEL:])


def _ffn_qg(h2d, ffn, gpre, w):
    rows = h2d.shape[0]
    row_spec = pl.BlockSpec((ROW_TILE, D_MODEL), lambda i: (i, 0))
    return pl.pallas_call(
        _ffn_qg_kernel,
        out_shape=(jax.ShapeDtypeStruct((rows, D_MODEL), F32),
                   jax.ShapeDtypeStruct((rows, D_MODEL), BF16),
                   jax.ShapeDtypeStruct((rows, D_MODEL), F32)),
        grid=(rows // ROW_TILE,),
        in_specs=[row_spec] + _ffn_specs() + [_resident((1, D_MODEL)), _resident((D_MODEL, 2 * D_MODEL))],
        out_specs=(row_spec, row_spec, row_spec),
        scratch_shapes=[_ACT_SCRATCH],
        compiler_params=pltpu.CompilerParams(
            dimension_semantics=("arbitrary",), vmem_limit_bytes=VMEM_LIMIT),
        name="ffn_qg",
    )(h2d, *ffn, gpre, w)


def _attn_schedule():
    t = ATT_BLOCK
    items = []
    for qi in range(SEQ // t):
        if qi % 2 == 0:
            blk = [(qi * t, t, 0)]
        else:
            blk = [((qi - 1) * t, 2 * t, t)]
        blk += [(2 * j * t, 2 * t, None) for j in range(qi // 2)]
        for n, (k0, nk, off) in enumerate(blk):
            items.append((qi, k0, nk, off, n == 0, n == len(blk) - 1))
    return items


def _attn_kernel(q_ref, qx_ref, k_ref, kx_ref, vt_ref, g_ref, o_ref, s0_ref, s1_ref, qcat_ref):
    t = ATT_BLOCK
    s_refs = (s0_ref, s1_ref)
    lane = lax.broadcasted_iota(jnp.int32, (t, LANES), 1)

    def load_queries(qi):
        q2 = q_ref[0, qi * t:(qi + 1) * t, :].astype(F32)
        qx2 = qx_ref[0, qi * t:(qi + 1) * t, :].astype(F32)
        for hd in range(2):
            in_head = (lane >= hd * FOX_HEAD_DIM) & (lane < (hd + 1) * FOX_HEAD_DIM)
            in_bias = (lane >= hd * BIAS_LANES) & (lane < (hd + 1) * BIAS_LANES)
            qcat_ref[hd, :, 0:LANES] = jnp.where(in_head, q2, 0.0).astype(BF16)
            qcat_ref[hd, :, LANES:2 * LANES] = jnp.where(in_bias, qx2, 0.0).astype(BF16)

    cq = ATT_QCHUNK
    nc = t // cq

    def visible(item, c):
        _, _, nk, off, _, _ = item
        return nk if off is None else min(nk, off + (c + 1) * cq)

    def produce(hd, item):
        _, k0, _, off, _, _ = item
        out = [None] * nc

        def chunk(c):
            nk = visible(item, c)
            kcat = jnp.concatenate([k_ref[0, k0:k0 + nk, :], kx_ref[0, k0:k0 + nk, :]], axis=1)
            st = _dot_nt(kcat, qcat_ref[hd, c * cq:(c + 1) * cq, :])
            if off is not None:
                key = lax.broadcasted_iota(jnp.int32, (nk, cq), 0)
                qry = lax.broadcasted_iota(jnp.int32, (nk, cq), 1) + c * cq
                st = jnp.where(key <= qry + off, st, -1e30)
            s_refs[hd][0:nk, c * cq:(c + 1) * cq] = st
            out[c] = jnp.max(st, axis=0, keepdims=True)

        return [functools.partial(chunk, c) for c in range(nc)], lambda: tuple(out)

    def consume(hd, item, state, mb):
        k0 = item[1]
        out = [None] * nc

        def chunk(c):
            nk = visible(item, c)
            m, acc = state[c]
            m_new = jnp.maximum(m, mb[c])
            p = jnp.exp2(s_refs[hd][0:nk, c * cq:(c + 1) * cq] - m_new).astype(BF16)
            vt = jnp.concatenate(
                [vt_ref[0, r // t, hd, :, r % t:r % t + cq] for r in range(k0, k0 + nk, cq)], axis=1)
            out[c] = (m_new, jnp.exp2(m - m_new) * acc + _dot(vt, p))

        return [functools.partial(chunk, c) for c in range(nc)], lambda: tuple(out)

    def interleave(a, b):
        for c in range(max(len(a), len(b))):
            if c < len(a):
                a[c]()
            if c < len(b):
                b[c]()

    def finalize(qi, states):
        for c in range(nc):
            ot = [st[c][1][0:FOX_HEAD_DIM, :] / st[c][1][FOX_HEAD_DIM:FOX_HEAD_DIM + 1, :] for st in states]
            o = jnp.concatenate(ot, axis=0).T
            r0 = qi * t + c * cq
            o_ref[0, r0:r0 + cq, :] = (o * g_ref[0, r0:r0 + cq, :]).astype(BF16)

    init = tuple((jnp.full((1, cq), -1e30, F32), jnp.zeros((VT_ROWS, cq), F32)) for _ in range(nc))

    items = _attn_schedule()
    st = [init, init]
    mb = [None, None]
    done0 = None
    for g, item in enumerate(items):
        qi, first = item[0], item[4]
        if first:
            load_queries(qi)
        p0, mb0 = produce(0, item)
        if g > 0:
            prev = items[g - 1]
            c1, st1 = consume(1, prev, st[1], mb[1])
            interleave(p0, c1)
            st[1] = st1()
            if prev[5]:
                finalize(prev[0], (done0, st[1]))
                st[1] = init
        else:
            interleave(p0, [])
        mb[0] = mb0()
        c0, st0 = consume(0, item, init if first else st[0], mb[0])
        p1, mb1 = produce(1, item)
        interleave(c0, p1)
        st[0] = st0()
        if item[5]:
            done0 = st[0]
        mb[1] = mb1()
    last = items[-1]
    c1, st1 = consume(1, last, st[1], mb[1])
    interleave(c1, [])
    finalize(last[0], (done0, st1()))


def _attn(q, qx, k, kx, vt, g):
    t = ATT_BLOCK
    nblk = SEQ // t
    ks = pl.BlockSpec((1, SEQ, LANES), lambda b, hp: (b, 0, hp))
    vts = pl.BlockSpec((1, nblk, 2, VT_ROWS, t), lambda b, hp: (b, 0, hp, 0, 0))
    return pl.pallas_call(
        _attn_kernel,
        out_shape=jax.ShapeDtypeStruct((BATCH, SEQ, D_MODEL), BF16),
        grid=(BATCH, FOX_PAIRS),
        in_specs=[ks, ks, ks, ks, vts, ks],
        out_specs=ks,
        scratch_shapes=[pltpu.VMEM((2 * t, t), F32), pltpu.VMEM((2 * t, t), F32),
                        pltpu.VMEM((2, t, 2 * LANES), BF16)],
        compiler_params=pltpu.CompilerParams(
            dimension_semantics=("arbitrary", "arbitrary"), vmem_limit_bytes=VMEM_LIMIT),
        name="fox_attn",
    )(q, qx, k, kx, vt, g)


def _out_ffn_ple_kernel(h_ref, a_ref, p_ref, w_ref, mpost_ref, f_gpre, f_win, f_wout, f_gpost,
                        gpre_ref, wg_ref, wp_ref, gpost_ref, o_ref, act_ref):
    hn = h_ref[...] + _rms(_dot(a_ref[...], w_ref[...]), mpost_ref[...])
    hn = _ffn_stage(hn, f_gpre, f_win, f_wout, f_gpost, act_ref)
    o_ref[...] = _ple_stage(hn, p_ref[...], gpre_ref, wg_ref, wp_ref, gpost_ref)


def _out_ffn_ple(h2d, a2d, p2d, p_row0, w, mpost, ffn, ple):
    rows = h2d.shape[0]
    row_spec = pl.BlockSpec((ROW_TILE, D_MODEL), lambda i: (i, 0))
    p_spec = pl.BlockSpec((ROW_TILE, PLE_DIM), lambda i: (p_row0 // ROW_TILE + i, 0))
    return pl.pallas_call(
        _out_ffn_ple_kernel,
        out_shape=jax.ShapeDtypeStruct((rows, D_MODEL), F32),
        grid=(rows // ROW_TILE,),
        in_specs=[row_spec, row_spec, p_spec, _resident((D_MODEL, D_MODEL)), _resident((1, D_MODEL))]
        + _ffn_specs() + _ple_specs(),
        out_specs=row_spec,
        scratch_shapes=[_ACT_SCRATCH],
        compiler_params=pltpu.CompilerParams(
            dimension_semantics=("arbitrary",), vmem_limit_bytes=VMEM_LIMIT),
        name="out_ffn_ple",
    )(h2d, a2d, p2d, w, mpost, *ffn, *ple)


def kernel(x, p, ffn1_norm_pre, ffn1_w_in, ffn1_w_out, ffn1_norm_post, mix_norm_pre, mix_norm_post, ffn2_norm_pre, ffn2_w_in, ffn2_w_out, ffn2_norm_post, hgrn_w_in, hgrn_lb_logits, hgrn_out_norm, hgrn_w_out, kv_norm, fox_w_kvf, fox_b_f, fox_w_qg, fox_w_out, ple_norm_pre, ple_w_gate, ple_w_proj, ple_norm_post):
    assert x.shape == (BATCH, SEQ, D_MODEL) and p.shape == (2, BATCH, SEQ, PLE_DIM)
    rows = BATCH * SEQ
    bf = lambda w: w.astype(BF16)
    vec = lambda g: g.reshape(1, -1).astype(F32)
    flat = lambda h: h.reshape(rows, D_MODEL)
    cube = lambda h: h.reshape(BATCH, SEQ, D_MODEL)
    p2 = p.reshape(2 * BATCH, SEQ, PLE_DIM)

    def ffn1(i):
        return vec(ffn1_norm_pre[i]), bf(ffn1_w_in[i]), bf(ffn1_w_out[i]), vec(ffn1_norm_post[i])

    def ffn2(i):
        return vec(ffn2_norm_pre[i]), bf(ffn2_w_in[i]), bf(ffn2_w_out[i]), vec(ffn2_norm_post[i])

    def ple(i):
        return vec(ple_norm_pre[i]), bf(ple_w_gate[i]), bf(ple_w_proj[i]), vec(ple_norm_post[i])

    h = _ffn(flat(x), ffn1(0))
    w_heads = bf(hgrn_w_in[0].reshape(D_MODEL, 4, HGRN_HEADS, HGRN_DK).transpose(2, 0, 1, 3)
                 .reshape(HGRN_HEADS, D_MODEL, 4 * HGRN_DK))
    h = _hgrn(cube(h), vec(mix_norm_pre[0]), w_heads, hgrn_lb_logits.astype(F32),
              vec(hgrn_out_norm[0]), bf(hgrn_w_out[0]), vec(mix_norm_post[0]))
    pad = LANES - 3 * FOX_HEADS
    wf3 = jnp.pad(jnp.tile(fox_w_kvf[:, 2 * D_MODEL:], (1, 3)), ((0, 0), (0, pad)))
    bf3 = jnp.pad(jnp.tile(fox_b_f, 3), (0, pad)).reshape(1, LANES).astype(F32)
    h, k_sh, kx_sh, qx_sh, vt_sh = _ffn_ple_kv(
        h, p2, ffn2(0), ple(0),
        vec(kv_norm), bf(fox_w_kvf[:, :D_MODEL]), bf(fox_w_kvf[:, D_MODEL:2 * D_MODEL].T), bf(wf3), bf3)

    h, q, g = _ffn_qg(flat(h), ffn1(1), vec(mix_norm_pre[1]), bf(fox_w_qg[0]))
    a = _attn(cube(q), qx_sh, k_sh, kx_sh, vt_sh, cube(g))
    h = _out_ffn_ple(h, flat(a), p2.reshape(2 * rows, PLE_DIM), rows, bf(fox_w_out[0]),
                     vec(mix_norm_post[1]), ffn2(1), ple(1))
    return cube(h)
```

```python
import functools

import numpy as np
import jax
import jax.numpy as jnp
from jax import lax
from jax.experimental import pallas as pl
from jax.experimental.pallas import tpu as pltpu

D_MODEL = 1024
BATCH = 8
SEQ = 4096
D_FF = 2816
PLE_DIM = 256
NORM_EPS = 1e-6
HGRN_HEADS = 8
HGRN_DK = 128
FOX_HEADS = 16
FOX_HEAD_DIM = 64
FOX_PAIRS = FOX_HEADS // 2

SUBLANES = 8
LANES = 128
MXU_DIM = 256

ROW_TILE = 512
FF_CHUNK = MXU_DIM
HGRN_CHUNK = 128
ATT_BLOCK = 512
ATT_QCHUNK = MXU_DIM
BIAS_LANES = 8
VT_ROWS = FOX_HEAD_DIM + 16
LOG2E = 1.4426950408889634
VMEM_LIMIT = 56 * 1024 * 1024

BF16 = jnp.bfloat16
F32 = jnp.float32


def _rms(x, g):
    return x * lax.rsqrt(jnp.mean(x * x, axis=-1, keepdims=True) + NORM_EPS) * g


def _dot(a, b):
    return jnp.dot(a, b, preferred_element_type=F32)


def _dot_nt(a, b):
    return lax.dot_general(a, b, (((1,), (1,)), ((), ())), preferred_element_type=F32)


def _dot_tn(a, b):
    return lax.dot_general(a, b, (((0,), (0,)), ((), ())), preferred_element_type=F32)


def _resident(shape):
    zeros = (0,) * len(shape)
    return pl.BlockSpec(shape, lambda *_: zeros, pipeline_mode=pl.Buffered(1))


def _split3(x):
    h1 = x.astype(BF16)
    r1 = x - h1.astype(F32)
    h2 = r1.astype(BF16)
    h3 = (r1 - h2.astype(F32)).astype(BF16)
    return h1, h2, h3


def _ffn_stage(x, gpre_ref, win_ref, wout_ref, gpost_ref, act_ref):
    xn = _rms(x, gpre_ref[...]).astype(BF16)
    for j in range(D_FF // FF_CHUNK):
        lo = j * FF_CHUNK
        gate = _dot(xn, win_ref[:, lo:lo + FF_CHUNK])
        up = _dot(xn, win_ref[:, D_FF + lo:D_FF + lo + FF_CHUNK])
        act_ref[:, lo:lo + FF_CHUNK] = (gate * jax.nn.sigmoid(gate) * up).astype(BF16)
    y = _dot(act_ref[...], wout_ref[...])
    return x + 0.5 * _rms(y, gpost_ref[...])


def _ffn_specs():
    return [_resident((1, D_MODEL)), _resident((D_MODEL, 2 * D_FF)),
            _resident((D_FF, D_MODEL)), _resident((1, D_MODEL))]


_ACT_SCRATCH = pltpu.VMEM((ROW_TILE, D_FF), BF16)


def _ffn_kernel(x_ref, gpre_ref, win_ref, wout_ref, gpost_ref, o_ref, act_ref):
    o_ref[...] = _ffn_stage(x_ref[...], gpre_ref, win_ref, wout_ref, gpost_ref, act_ref)


def _ffn(h2d, ffn):
    rows = h2d.shape[0]
    row_spec = pl.BlockSpec((ROW_TILE, D_MODEL), lambda i: (i, 0))
    return pl.pallas_call(
        _ffn_kernel,
        out_shape=jax.ShapeDtypeStruct((rows, D_MODEL), F32),
        grid=(rows // ROW_TILE,),
        in_specs=[row_spec] + _ffn_specs(),
        out_specs=row_spec,
        scratch_shapes=[_ACT_SCRATCH],
        compiler_params=pltpu.CompilerParams(
            dimension_semantics=("arbitrary",), vmem_limit_bytes=VMEM_LIMIT),
        name="ffn",
    )(h2d, *ffn)


def _level_table(c):
    t = np.arange(c)[:, None]
    s = np.arange(c)[None, :]
    x = np.maximum(t ^ s, 1)
    lv = np.floor(np.log2(x)).astype(np.int32)
    lv = np.where(s == t, -1, lv)
    lv = np.where(s > t, -2, lv)
    return lv.astype(np.int32)


def _hgrn_chunk(q, z, v, lb, oml, st, lv, tril):
    c = HGRN_CHUNK
    nt = c // SUBLANES
    e = jnp.exp(-jnp.abs(z))
    r = 1.0 / (1.0 + e)
    er = e * r
    pos = z >= 0
    f = lb + oml * jnp.where(pos, r, er)
    kk = oml * jnp.where(pos, er, r)
    lf = jnp.log2(f)

    h1, h2, h3 = _split3(lf)
    cum = _dot(tril, h1) + _dot(tril, h2) + _dot(tril, h3)
    yield None

    sub = lax.broadcasted_iota(jnp.int32, (SUBLANES, LANES), 0)
    zero = jnp.zeros((SUBLANES, LANES), F32)
    tiles = [cum[SUBLANES * j:SUBLANES * (j + 1), :] for j in range(nt)]

    def row(j, rr):
        return jnp.broadcast_to(cum[SUBLANES * j + rr:SUBLANES * j + rr + 1, :], (SUBLANES, LANES))

    b1 = [row(j, 1) for j in range(nt)]
    b3 = [row(j, 3) for j in range(nt)]
    b5 = [row(j, 5) for j in range(nt)]
    b7 = [row(j, 7) for j in range(nt)]
    qt = [q[SUBLANES * j:SUBLANES * (j + 1), :] for j in range(nt)]
    kt = [kk[SUBLANES * j:SUBLANES * (j + 1), :] for j in range(nt)]

    def cat(ts):
        return jnp.concatenate(ts, axis=0).astype(BF16)

    v_bf = v.astype(BF16)
    kk_bf = kk.astype(BF16)
    sc = jnp.where(lv == -1, _dot_nt(q.astype(BF16), kk_bf), 0.0)
    sc = jnp.where(lv == 0, _dot_nt((q * f).astype(BF16), kk_bf), sc)
    yield None

    level = 1
    blk = 2
    while blk < c:
        qs, ks = [], []
        for j in range(nt):
            if blk < SUBLANES:
                ref = jnp.where(sub < 4, b1[j], b5[j]) if blk == 2 else b3[j]
                e = jnp.exp2(-jnp.abs(tiles[j] - ref))
                qs.append(qt[j] * e)
                ks.append(kt[j] * e)
            else:
                m = blk // SUBLANES
                b = j // m
                if b % 2 == 1:
                    qs.append(qt[j] * jnp.exp2(tiles[j] - b7[b * m - 1]))
                    ks.append(zero)
                else:
                    qs.append(zero)
                    ks.append(kt[j] * jnp.exp2(b7[b * m + m - 1] - tiles[j]))
        sc = jnp.where(lv == level, _dot_nt(cat(qs), cat(ks)), sc)
        if blk in (4, 16):
            yield None
        level += 1
        blk *= 2

    last = b7[nt - 1]
    qe = cat([qt[j] * jnp.exp2(tiles[j]) for j in range(nt)])
    ke = cat([kt[j] * jnp.exp2(last - tiles[j]) for j in range(nt)])
    yield None
    o = _dot_nt(qe, st.astype(BF16)) + _dot(sc.astype(BF16), v_bf)
    st_new = st * jnp.exp2(last[0:1, :]) + _dot_tn(v_bf, ke)
    yield o, st_new


def _hgrn_kernel(h_ref, gpre_ref, w_ref, lbl_ref, onorm_ref, wout_ref, gpost_ref, lv_ref, tril_ref,
                 o_ref, xn_ref, p_ref, og_ref, st_ref):
    c = HGRN_CHUNK

    @pl.when(pl.program_id(0) == 0)
    def _():
        st_ref[...] = jnp.zeros_like(st_ref)

    gpre = gpre_ref[...]
    for b in range(BATCH):
        xn_ref[b * c:(b + 1) * c, :] = _rms(h_ref[b], gpre).astype(BF16)

    l0 = lbl_ref[0:1, :]
    l1 = lbl_ref[1:2, :]
    mx = jnp.maximum(l0, l1)
    e0 = jnp.exp(l0 - mx)
    e1 = jnp.exp(l1 - mx)
    lb_all = e0 / (e0 + e1)
    onorm = onorm_ref[...]

    def chunk_steps(head, b):
        lb = lb_all[:, head * HGRN_DK:(head + 1) * HGRN_DK]
        blk = p_ref[head % 2, b * c:(b + 1) * c, :]
        q = blk[:, 0:HGRN_DK]
        z = blk[:, HGRN_DK:2 * HGRN_DK]
        v = blk[:, 2 * HGRN_DK:3 * HGRN_DK]
        gg = blk[:, 3 * HGRN_DK:4 * HGRN_DK]
        gate = gg * jax.nn.sigmoid(gg)
        idx = head * BATCH + b
        for res in _hgrn_chunk(q, z, v, lb, 1.0 - lb, st_ref[idx], lv_ref[...], tril_ref[...]):
            if res is None:
                yield
        o, st_new = res
        st_ref[idx] = st_new
        og_ref[b * c:(b + 1) * c, head * HGRN_DK:(head + 1) * HGRN_DK] = (_rms(o, onorm) * gate).astype(BF16)

    p_ref[0] = _dot(xn_ref[...], w_ref[0])
    todo = [(head, b) for head in range(HGRN_HEADS) for b in range(BATCH)]
    live = []
    while todo or live:
        if todo:
            head, b = todo.pop(0)
            if b == 0 and head + 1 < HGRN_HEADS:
                p_ref[(head + 1) % 2] = _dot(xn_ref[...], w_ref[head + 1])
            live.append(chunk_steps(head, b))
        live = [g for g in live if next(g, StopIteration) is not StopIteration]

    y = _dot(og_ref[...], wout_ref[...])
    gpost = gpost_ref[...]
    for b in range(BATCH):
        o_ref[b] = h_ref[b] + _rms(y[b * c:(b + 1) * c, :], gpost)


def _hgrn(h, gpre, w_heads, lb_logits, onorm, wout, gpost):
    c = HGRN_CHUNK
    lv = jnp.asarray(_level_table(c))
    tril = jnp.asarray(np.tril(np.ones((c, c), np.float32)), dtype=BF16)
    blk = pl.BlockSpec((BATCH, c, D_MODEL), lambda j: (0, j, 0))
    return pl.pallas_call(
        _hgrn_kernel,
        out_shape=jax.ShapeDtypeStruct((BATCH, SEQ, D_MODEL), F32),
        grid=(SEQ // c,),
        in_specs=[blk, _resident((1, D_MODEL)), _resident((HGRN_HEADS, D_MODEL, 4 * HGRN_DK)),
                  _resident((2, D_MODEL)), _resident((1, HGRN_DK)), _resident((D_MODEL, D_MODEL)),
                  _resident((1, D_MODEL)), _resident((c, c)), _resident((c, c))],
        out_specs=blk,
        scratch_shapes=[pltpu.VMEM((BATCH * c, D_MODEL), BF16),
                        pltpu.VMEM((2, BATCH * c, 4 * HGRN_DK), F32),
                        pltpu.VMEM((BATCH * c, D_MODEL), BF16),
                        pltpu.VMEM((HGRN_HEADS * BATCH, HGRN_DK, HGRN_DK), F32)],
        compiler_params=pltpu.CompilerParams(
            dimension_semantics=("arbitrary",), vmem_limit_bytes=VMEM_LIMIT),
        name="hgrn",
    )(h, gpre, w_heads, lb_logits, onorm, wout, gpost, lv, tril)


def _ple_stage(x, p, gpre_ref, wg_ref, wp_ref, gpost_ref):
    xn = _rms(x, gpre_ref[...]).astype(BF16)
    gate = jax.nn.sigmoid(_dot(xn, wg_ref[...]))
    pp = _dot(p.astype(BF16), wp_ref[...])
    return x + _rms(gate * pp, gpost_ref[...])


def _ple_specs():
    return [_resident((1, D_MODEL)), _resident((D_MODEL, D_MODEL)),
            _resident((PLE_DIM, D_MODEL)), _resident((1, D_MODEL))]


def _ffn_ple_kv_kernel(h_ref, p_ref, f_gpre, f_win, f_wout, f_gpost, gpre_ref, wg_ref, wp_ref, gpost_ref,
                       kvn_ref, wk_ref, wvt_ref, wf_ref, bf_ref, tril_ref, ek_ref, eq_ref,
                       o_ref, k_ref, kx_ref, qx_ref, vt_ref, act_ref, carry_ref):
    hn = _ffn_stage(h_ref[0], f_gpre, f_win, f_wout, f_gpost, act_ref)
    hn = _ple_stage(hn, p_ref[0], gpre_ref, wg_ref, wp_ref, gpost_ref)
    o_ref[0] = hn
    xn = _rms(hn, kvn_ref[...]).astype(BF16)
    k_ref[0] = _dot(xn, wk_ref[...]).astype(BF16)
    vt = _dot_nt(wvt_ref[...], xn).astype(BF16)
    ones = jnp.ones((VT_ROWS - FOX_HEAD_DIM, ROW_TILE), BF16)
    for hd in range(FOX_HEADS):
        vt_ref[0, 0, hd, 0:FOX_HEAD_DIM, :] = vt[hd * FOX_HEAD_DIM:(hd + 1) * FOX_HEAD_DIM, :]
        vt_ref[0, 0, hd, FOX_HEAD_DIM:VT_ROWS, :] = ones

    @pl.when(pl.program_id(1) == 0)
    def _():
        carry_ref[...] = jnp.zeros_like(carry_ref)

    x = _dot(xn, wf_ref[...]) + bf_ref[...]
    ls = jnp.minimum(x, 0.0) - jnp.log1p(jnp.exp(-jnp.abs(x)))
    h1, h2, h3 = _split3(ls)
    tril = tril_ref[...]
    cum = _dot(tril, h1) + _dot(tril, h2) + _dot(tril, h3) + carry_ref[0:1, :]
    carry_ref[...] = jnp.broadcast_to(cum[ROW_TILE - 1:ROW_TILE, :], carry_ref.shape)
    c2 = cum * LOG2E
    hi = c2.astype(BF16).astype(F32)
    r1 = c2 - hi
    mid = r1.astype(BF16).astype(F32)
    lo = r1 - mid
    lane = lax.broadcasted_iota(jnp.int32, c2.shape, 1)
    cx = jnp.where(lane < FOX_HEADS, hi, jnp.where(lane < 2 * FOX_HEADS, mid, jnp.where(
        lane < 3 * FOX_HEADS, lo, jnp.where(lane == 3 * FOX_HEADS, 1.0, 0.0)))).astype(BF16)
    kx_ref[0] = _dot(cx, ek_ref[...]).astype(BF16)
    qx_ref[0] = _dot(cx, eq_ref[...]).astype(BF16)


def _bias_placement():
    ek = np.zeros((LANES, D_MODEL), np.float32)
    eq = np.zeros((LANES, D_MODEL), np.float32)
    one_lane = 3 * FOX_HEADS
    for head in range(FOX_HEADS):
        base = (head // 2) * LANES + (head % 2) * BIAS_LANES
        for i in range(3):
            ek[i * FOX_HEADS + head, base + i] = 1.0
            eq[one_lane, base + i] = -1.0
            ek[one_lane, base + 3 + i] = 1.0
            eq[i * FOX_HEADS + head, base + 3 + i] = 1.0
    return jnp.asarray(ek, dtype=BF16), jnp.asarray(eq, dtype=BF16)


def _ffn_ple_kv(h, p, ffn, ple, kvn, wk, wvt, wf3, bf3):
    assert ROW_TILE == ATT_BLOCK
    nblk = SEQ // ROW_TILE
    hs = pl.BlockSpec((1, ROW_TILE, D_MODEL), lambda b, i: (b, i, 0))
    ps = pl.BlockSpec((1, ROW_TILE, PLE_DIM), lambda b, i: (b, i, 0))
    vts = pl.BlockSpec((1, 1, FOX_HEADS, VT_ROWS, ROW_TILE), lambda b, i: (b, i, 0, 0, 0))
    tril = jnp.asarray(np.tril(np.ones((ROW_TILE, ROW_TILE), np.float32)), dtype=BF16)
    ek, eq = _bias_placement()
    return pl.pallas_call(
        _ffn_ple_kv_kernel,
        out_shape=(jax.ShapeDtypeStruct((BATCH, SEQ, D_MODEL), F32),
                   jax.ShapeDtypeStruct((BATCH, SEQ, D_MODEL), BF16),
                   jax.ShapeDtypeStruct((BATCH, SEQ, D_MODEL), BF16),
                   jax.ShapeDtypeStruct((BATCH, SEQ, D_MODEL), BF16),
                   jax.ShapeDtypeStruct((BATCH, nblk, FOX_HEADS, VT_ROWS, ROW_TILE), BF16)),
        grid=(BATCH, nblk),
        in_specs=[hs, ps] + _ffn_specs() + _ple_specs() + [
            _resident((1, D_MODEL)), _resident((D_MODEL, D_MODEL)), _resident((D_MODEL, D_MODEL)),
            _resident((D_MODEL, LANES)), _resident((1, LANES)),
            _resident((ROW_TILE, ROW_TILE)), _resident((LANES, D_MODEL)), _resident((LANES, D_MODEL))],
        out_specs=(hs, hs, hs, hs, vts),
        scratch_shapes=[_ACT_SCRATCH, pltpu.VMEM((SUBLANES, LANES), F32)],
        compiler_params=pltpu.CompilerParams(
            dimension_semantics=("arbitrary", "arbitrary"), vmem_limit_bytes=VMEM_LIMIT),
        name="ffn_ple_kv",
    )(h, p, *ffn, *ple, kvn, wk, wvt, wf3, bf3, tril, ek, eq)


def _ffn_qg_kernel(h_ref, f_gpre, f_win, f_wout, f_gpost, gpre_ref, w_ref, o_ref, q_ref, g_ref, act_ref):
    hn = _ffn_stage(h_ref[...], f_gpre, f_win, f_wout, f_gpost, act_ref)
    o_ref[...] = hn
    xn = _rms(hn, gpre_ref[...]).astype(BF16)
    qg = _dot(xn, w_ref[...])
    q_ref[...] = (qg[:, :D_MODEL] * (FOX_HEAD_DIM ** -0.5 * LOG2E)).astype(BF16)
    g_ref[...] = jax.nn.sigmoid(qg[:, D_MODEL:])


def _ffn_qg(h2d, ffn, gpre, w):
    rows = h2d.shape[0]
    row_spec = pl.BlockSpec((ROW_TILE, D_MODEL), lambda i: (i, 0))
    return pl.pallas_call(
        _ffn_qg_kernel,
        out_shape=(jax.ShapeDtypeStruct((rows, D_MODEL), F32),
                   jax.ShapeDtypeStruct((rows, D_MODEL), BF16),
                   jax.ShapeDtypeStruct((rows, D_MODEL), F32)),
        grid=(rows // ROW_TILE,),
        in_specs=[row_spec] + _ffn_specs() + [_resident((1, D_MODEL)), _resident((D_MODEL, 2 * D_MODEL))],
        out_specs=(row_spec, row_spec, row_spec),
        scratch_shapes=[_ACT_SCRATCH],
        compiler_params=pltpu.CompilerParams(
            dimension_semantics=("arbitrary",), vmem_limit_bytes=VMEM_LIMIT),
        name="ffn_qg",
    )(h2d, *ffn, gpre, w)


def _attn_schedule():
    t = ATT_BLOCK
    items = []
    for qi in range(SEQ // t):
        if qi % 2 == 0:
            blk = [(qi * t, t, 0)]
        else:
            blk = [((qi - 1) * t, 2 * t, t)]
        blk += [(2 * j * t, 2 * t, None) for j in range(qi // 2)]
        for n, (k0, nk, off) in enumerate(blk):
            items.append((qi, k0, nk, off, n == 0, n == len(blk) - 1))
    return items


def _attn_kernel(q_ref, qx_ref, k_ref, kx_ref, vt_ref, g_ref, o_ref, s0_ref, s1_ref, qcat_ref):
    t = ATT_BLOCK
    s_refs = (s0_ref, s1_ref)
    lane = lax.broadcasted_iota(jnp.int32, (t, LANES), 1)

    def load_queries(qi):
        q2 = q_ref[0, qi * t:(qi + 1) * t, :].astype(F32)
        qx2 = qx_ref[0, qi * t:(qi + 1) * t, :].astype(F32)
        for hd in range(2):
            in_head = (lane >= hd * FOX_HEAD_DIM) & (lane < (hd + 1) * FOX_HEAD_DIM)
            in_bias = (lane >= hd * BIAS_LANES) & (lane < (hd + 1) * BIAS_LANES)
            qcat_ref[hd, :, 0:LANES] = jnp.where(in_head, q2, 0.0).astype(BF16)
            qcat_ref[hd, :, LANES:2 * LANES] = jnp.where(in_bias, qx2, 0.0).astype(BF16)

    cq = ATT_QCHUNK
    nc = t // cq

    def visible(item, c):
        _, _, nk, off, _, _ = item
        return nk if off is None else min(nk, off + (c + 1) * cq)

    def produce(hd, item):
        _, k0, _, off, _, _ = item
        out = [None] * nc

        def chunk(c):
            nk = visible(item, c)
            kcat = jnp.concatenate([k_ref[0, k0:k0 + nk, :], kx_ref[0, k0:k0 + nk, :]], axis=1)
            st = _dot_nt(kcat, qcat_ref[hd, c * cq:(c + 1) * cq, :])
            if off is not None:
                key = lax.broadcasted_iota(jnp.int32, (nk, cq), 0)
                qry = lax.broadcasted_iota(jnp.int32, (nk, cq), 1) + c * cq
                st = jnp.where(key <= qry + off, st, -1e30)
            s_refs[hd][0:nk, c * cq:(c + 1) * cq] = st
            out[c] = jnp.max(st, axis=0, keepdims=True)

        return [functools.partial(chunk, c) for c in range(nc)], lambda: tuple(out)

    def consume(hd, item, state, mb):
        k0 = item[1]
        out = [None] * nc

        def chunk(c):
            nk = visible(item, c)
            m, acc = state[c]
            m_new = jnp.maximum(m, mb[c])
            p = jnp.exp2((s_refs[hd][0:nk, c * cq:(c + 1) * cq] - m_new).astype(BF16))
            vt = jnp.concatenate(
                [vt_ref[0, r // t, hd, :, r % t:r % t + cq] for r in range(k0, k0 + nk, cq)], axis=1)
            out[c] = (m_new, jnp.exp2(m - m_new) * acc + _dot(vt, p))

        return [functools.partial(chunk, c) for c in range(nc)], lambda: tuple(out)

    def interleave(a, b):
        for c in range(max(len(a), len(b))):
            if c < len(a):
                a[c]()
            if c < len(b):
                b[c]()

    def finalize(qi, states):
        for c in range(nc):
            ot = [st[c][1][0:FOX_HEAD_DIM, :] / st[c][1][FOX_HEAD_DIM:FOX_HEAD_DIM + 1, :] for st in states]
            o = jnp.concatenate(ot, axis=0).T
            r0 = qi * t + c * cq
            o_ref[0, r0:r0 + cq, :] = (o * g_ref[0, r0:r0 + cq, :]).astype(BF16)

    init = tuple((jnp.full((1, cq), -1e30, F32), jnp.zeros((VT_ROWS, cq), F32)) for _ in range(nc))

    items = _attn_schedule()
    st = [init, init]
    mb = [None, None]
    done0 = None
    for g, item in enumerate(items):
        qi, first = item[0], item[4]
        if first:
            load_queries(qi)
        p0, mb0 = produce(0, item)
        if g > 0:
            prev = items[g - 1]
            c1, st1 = consume(1, prev, st[1], mb[1])
            interleave(p0, c1)
            st[1] = st1()
            if prev[5]:
                finalize(prev[0], (done0, st[1]))
                st[1] = init
        else:
            interleave(p0, [])
        mb[0] = mb0()
        c0, st0 = consume(0, item, init if first else st[0], mb[0])
        p1, mb1 = produce(1, item)
        interleave(c0, p1)
        st[0] = st0()
        if item[5]:
            done0 = st[0]
        mb[1] = mb1()
    last = items[-1]
    c1, st1 = consume(1, last, st[1], mb[1])
    interleave(c1, [])
    finalize(last[0], (done0, st1()))


def _attn(q, qx, k, kx, vt, g):
    t = ATT_BLOCK
    nblk = SEQ // t
    ks = pl.BlockSpec((1, SEQ, LANES), lambda b, hp: (b, 0, hp))
    vts = pl.BlockSpec((1, nblk, 2, VT_ROWS, t), lambda b, hp: (b, 0, hp, 0, 0))
    return pl.pallas_call(
        _attn_kernel,
        out_shape=jax.ShapeDtypeStruct((BATCH, SEQ, D_MODEL), BF16),
        grid=(BATCH, FOX_PAIRS),
        in_specs=[ks, ks, ks, ks, vts, ks],
        out_specs=ks,
        scratch_shapes=[pltpu.VMEM((2 * t, t), F32), pltpu.VMEM((2 * t, t), F32),
                        pltpu.VMEM((2, t, 2 * LANES), BF16)],
        compiler_params=pltpu.CompilerParams(
            dimension_semantics=("arbitrary", "arbitrary"), vmem_limit_bytes=VMEM_LIMIT),
        name="fox_attn",
    )(q, qx, k, kx, vt, g)


def _out_ffn_ple_kernel(h_ref, a_ref, p_ref, w_ref, mpost_ref, f_gpre, f_win, f_wout, f_gpost,
                        gpre_ref, wg_ref, wp_ref, gpost_ref, o_ref, act_ref):
    hn = h_ref[...] + _rms(_dot(a_ref[...], w_ref[...]), mpost_ref[...])
    hn = _ffn_stage(hn, f_gpre, f_win, f_wout, f_gpost, act_ref)
    o_ref[...] = _ple_stage(hn, p_ref[...], gpre_ref, wg_ref, wp_ref, gpost_ref)


def _out_ffn_ple(h2d, a2d, p2d, p_row0, w, mpost, ffn, ple):
    rows = h2d.shape[0]
    row_spec = pl.BlockSpec((ROW_TILE, D_MODEL), lambda i: (i, 0))
    p_spec = pl.BlockSpec((ROW_TILE, PLE_DIM), lambda i: (p_row0 // ROW_TILE + i, 0))
    return pl.pallas_call(
        _out_ffn_ple_kernel,
        out_shape=jax.ShapeDtypeStruct((rows, D_MODEL), F32),
        grid=(rows // ROW_TILE,),
        in_specs=[row_spec, row_spec, p_spec, _resident((D_MODEL, D_MODEL)), _resident((1, D_MODEL))]
        + _ffn_specs() + _ple_specs(),
        out_specs=row_spec,
        scratch_shapes=[_ACT_SCRATCH],
        compiler_params=pltpu.CompilerParams(
            dimension_semantics=("arbitrary",), vmem_limit_bytes=VMEM_LIMIT),
        name="out_ffn_ple",
    )(h2d, a2d, p2d, w, mpost, *ffn, *ple)


def kernel(x, p, ffn1_norm_pre, ffn1_w_in, ffn1_w_out, ffn1_norm_post, mix_norm_pre, mix_norm_post, ffn2_norm_pre, ffn2_w_in, ffn2_w_out, ffn2_norm_post, hgrn_w_in, hgrn_lb_logits, hgrn_out_norm, hgrn_w_out, kv_norm, fox_w_kvf, fox_b_f, fox_w_qg, fox_w_out, ple_norm_pre, ple_w_gate, ple_w_proj, ple_norm_post):
    assert x.shape == (BATCH, SEQ, D_MODEL) and p.shape == (2, BATCH, SEQ, PLE_DIM)
    rows = BATCH * SEQ
    bf = lambda w: w.astype(BF16)
    vec = lambda g: g.reshape(1, -1).astype(F32)
    flat = lambda h: h.reshape(rows, D_MODEL)
    cube = lambda h: h.reshape(BATCH, SEQ, D_MODEL)
    p2 = p.reshape(2 * BATCH, SEQ, PLE_DIM)

    def ffn1(i):
        return vec(ffn1_norm_pre[i]), bf(ffn1_w_in[i]), bf(ffn1_w_out[i]), vec(ffn1_norm_post[i])

    def ffn2(i):
        return vec(ffn2_norm_pre[i]), bf(ffn2_w_in[i]), bf(ffn2_w_out[i]), vec(ffn2_norm_post[i])

    def ple(i):
        return vec(ple_norm_pre[i]), bf(ple_w_gate[i]), bf(ple_w_proj[i]), vec(ple_norm_post[i])

    h = _ffn(flat(x), ffn1(0))
    w_heads = bf(hgrn_w_in[0].reshape(D_MODEL, 4, HGRN_HEADS, HGRN_DK).transpose(2, 0, 1, 3)
                 .reshape(HGRN_HEADS, D_MODEL, 4 * HGRN_DK))
    h = _hgrn(cube(h), vec(mix_norm_pre[0]), w_heads, hgrn_lb_logits.astype(F32),
              vec(hgrn_out_norm[0]), bf(hgrn_w_out[0]), vec(mix_norm_post[0]))
    pad = LANES - 3 * FOX_HEADS
    wf3 = jnp.pad(jnp.tile(fox_w_kvf[:, 2 * D_MODEL:], (1, 3)), ((0, 0), (0, pad)))
    bf3 = jnp.pad(jnp.tile(fox_b_f, 3), (0, pad)).reshape(1, LANES).astype(F32)
    h, k_sh, kx_sh, qx_sh, vt_sh = _ffn_ple_kv(
        h, p2, ffn2(0), ple(0),
        vec(kv_norm), bf(fox_w_kvf[:, :D_MODEL]), bf(fox_w_kvf[:, D_MODEL:2 * D_MODEL].T), bf(wf3), bf3)

    h, q, g = _ffn_qg(flat(h), ffn1(1), vec(mix_norm_pre[1]), bf(fox_w_qg[0]))
    a = _attn(cube(q), qx_sh, k_sh, kx_sh, vt_sh, cube(g))
    h = _out_ffn_ple(h, flat(a), p2.reshape(2 * rows, PLE_DIM), rows, bf(fox_w_out[0]),
                     vec(mix_norm_post[1]), ffn2(1), ple(1))
    return cube(h)
```

```python
import functools

import numpy as np
import jax
import jax.numpy as jnp
from jax import lax
from jax.experimental import pallas as pl
from jax.experimental.pallas import tpu as pltpu

D_MODEL = 1024
BATCH = 8
SEQ = 4096
D_FF = 2816
PLE_DIM = 256
NORM_EPS = 1e-6
HGRN_HEADS = 8
HGRN_DK = 128
FOX_HEADS = 16
FOX_HEAD_DIM = 64
FOX_PAIRS = FOX_HEADS // 2

SUBLANES = 8
LANES = 128
MXU_DIM = 256

ROW_TILE = 512
FF_CHUNK = MXU_DIM
HGRN_CHUNK = 128
ATT_BLOCK = 512
ATT_QCHUNK = MXU_DIM
BIAS_LANES = 8
VT_ROWS = FOX_HEAD_DIM + 16
LOG2E = 1.4426950408889634
VMEM_LIMIT = 56 * 1024 * 1024

BF16 = jnp.bfloat16
F32 = jnp.float32


def _rms(x, g):
    return x * lax.rsqrt(jnp.mean(x * x, axis=-1, keepdims=True) + NORM_EPS) * g


def _dot(a, b):
    return jnp.dot(a, b, preferred_element_type=F32)


def _dot_nt(a, b):
    return lax.dot_general(a, b, (((1,), (1,)), ((), ())), preferred_element_type=F32)


def _dot_tn(a, b):
    return lax.dot_general(a, b, (((0,), (0,)), ((), ())), preferred_element_type=F32)


def _resident(shape):
    zeros = (0,) * len(shape)
    return pl.BlockSpec(shape, lambda *_: zeros, pipeline_mode=pl.Buffered(1))


def _split3(x):
    h1 = x.astype(BF16)
    r1 = x - h1.astype(F32)
    h2 = r1.astype(BF16)
    h3 = (r1 - h2.astype(F32)).astype(BF16)
    return h1, h2, h3


def _ffn_stage(x, gpre_ref, win_ref, wout_ref, gpost_ref, act_ref):
    xn = _rms(x, gpre_ref[...]).astype(BF16)
    for j in range(D_FF // FF_CHUNK):
        lo = j * FF_CHUNK
        gate = _dot(xn, win_ref[:, lo:lo + FF_CHUNK])
        up = _dot(xn, win_ref[:, D_FF + lo:D_FF + lo + FF_CHUNK])
        act_ref[:, lo:lo + FF_CHUNK] = (gate * jax.nn.sigmoid(gate) * up).astype(BF16)
    y = _dot(act_ref[...], wout_ref[...])
    return x + 0.5 * _rms(y, gpost_ref[...])


def _layer_resident(shape, layer):
    return pl.BlockSpec((None,) + shape, lambda *_: (layer, 0, 0), pipeline_mode=pl.Buffered(1))


def _ffn_specs(layer):
    return [_resident((1, D_MODEL)), _layer_resident((D_MODEL, 2 * D_FF), layer),
            _layer_resident((D_FF, D_MODEL), layer), _resident((1, D_MODEL))]


def _cast_kernel(w_ref, o_ref):
    o_ref[...] = w_ref[...].astype(BF16)


def _cast_bf16(w):
    layers, rows, cols = w.shape
    rb = rows // 4
    spec = pl.BlockSpec((1, rb, cols), lambda l, i: (l, i, 0))
    return pl.pallas_call(
        _cast_kernel,
        out_shape=jax.ShapeDtypeStruct(w.shape, BF16),
        grid=(layers, rows // rb),
        in_specs=[spec],
        out_specs=spec,
        compiler_params=pltpu.CompilerParams(
            dimension_semantics=("arbitrary", "arbitrary"), vmem_limit_bytes=VMEM_LIMIT),
        name="cast_bf16",
    )(w)


_ACT_SCRATCH = pltpu.VMEM((ROW_TILE, D_FF), BF16)


def _ffn_kernel(x_ref, gpre_ref, win_ref, wout_ref, gpost_ref, o_ref, act_ref):
    o_ref[...] = _ffn_stage(x_ref[...], gpre_ref, win_ref, wout_ref, gpost_ref, act_ref)


def _ffn(h2d, ffn):
    rows = h2d.shape[0]
    row_spec = pl.BlockSpec((ROW_TILE, D_MODEL), lambda i: (i, 0))
    return pl.pallas_call(
        _ffn_kernel,
        out_shape=jax.ShapeDtypeStruct((rows, D_MODEL), F32),
        grid=(rows // ROW_TILE,),
        in_specs=[row_spec] + _ffn_specs(ffn[0]),
        out_specs=row_spec,
        scratch_shapes=[_ACT_SCRATCH],
        compiler_params=pltpu.CompilerParams(
            dimension_semantics=("arbitrary",), vmem_limit_bytes=VMEM_LIMIT),
        name="ffn",
    )(h2d, *ffn[1])


def _level_table(c):
    t = np.arange(c)[:, None]
    s = np.arange(c)[None, :]
    x = np.maximum(t ^ s, 1)
    lv = np.floor(np.log2(x)).astype(np.int32)
    lv = np.where(s == t, -1, lv)
    lv = np.where(s > t, -2, lv)
    return lv.astype(np.int32)


def _hgrn_chunk(q, z, v, lb, oml, st, lv, tril):
    c = HGRN_CHUNK
    nt = c // SUBLANES
    e = jnp.exp(-jnp.abs(z))
    r = 1.0 / (1.0 + e)
    er = e * r
    pos = z >= 0
    f = lb + oml * jnp.where(pos, r, er)
    kk = oml * jnp.where(pos, er, r)
    lf = jnp.log2(f)

    h1, h2, h3 = _split3(lf)
    cum = _dot(tril, h1) + _dot(tril, h2) + _dot(tril, h3)
    yield None

    sub = lax.broadcasted_iota(jnp.int32, (SUBLANES, LANES), 0)
    zero = jnp.zeros((SUBLANES, LANES), F32)
    tiles = [cum[SUBLANES * j:SUBLANES * (j + 1), :] for j in range(nt)]

    def row(j, rr):
        return jnp.broadcast_to(cum[SUBLANES * j + rr:SUBLANES * j + rr + 1, :], (SUBLANES, LANES))

    b1 = [row(j, 1) for j in range(nt)]
    b3 = [row(j, 3) for j in range(nt)]
    b5 = [row(j, 5) for j in range(nt)]
    b7 = [row(j, 7) for j in range(nt)]
    qt = [q[SUBLANES * j:SUBLANES * (j + 1), :] for j in range(nt)]
    kt = [kk[SUBLANES * j:SUBLANES * (j + 1), :] for j in range(nt)]

    def cat(ts):
        return jnp.concatenate(ts, axis=0).astype(BF16)

    v_bf = v.astype(BF16)
    kk_bf = kk.astype(BF16)
    sc = jnp.where(lv == -1, _dot_nt(q.astype(BF16), kk_bf), 0.0)
    sc = jnp.where(lv == 0, _dot_nt((q * f).astype(BF16), kk_bf), sc)
    yield None

    level = 1
    blk = 2
    while blk < c:
        qs, ks = [], []
        for j in range(nt):
            if blk < SUBLANES:
                ref = jnp.where(sub < 4, b1[j], b5[j]) if blk == 2 else b3[j]
                e = jnp.exp2(-jnp.abs(tiles[j] - ref))
                qs.append(qt[j] * e)
                ks.append(kt[j] * e)
            else:
                m = blk // SUBLANES
                b = j // m
                if b % 2 == 1:
                    qs.append(qt[j] * jnp.exp2(tiles[j] - b7[b * m - 1]))
                    ks.append(zero)
                else:
                    qs.append(zero)
                    ks.append(kt[j] * jnp.exp2(b7[b * m + m - 1] - tiles[j]))
        sc = jnp.where(lv == level, _dot_nt(cat(qs), cat(ks)), sc)
        if blk < c // 2:
            yield None
        level += 1
        blk *= 2

    last = b7[nt - 1]
    qe = cat([qt[j] * jnp.exp2(tiles[j]) for j in range(nt)])
    ke = cat([kt[j] * jnp.exp2(last - tiles[j]) for j in range(nt)])
    yield None
    o = _dot_nt(qe, st.astype(BF16)) + _dot(sc.astype(BF16), v_bf)
    st_new = st * jnp.exp2(last[0:1, :]) + _dot_tn(v_bf, ke)
    yield o, st_new


def _hgrn_kernel(h_ref, gpre_ref, w_ref, lbl_ref, onorm_ref, wout_ref, gpost_ref, lv_ref, tril_ref,
                 o_ref, xn_ref, p_ref, og_ref, st_ref):
    c = HGRN_CHUNK

    @pl.when(pl.program_id(0) == 0)
    def _():
        st_ref[...] = jnp.zeros_like(st_ref)

    gpre = gpre_ref[...]
    for b in range(BATCH):
        xn_ref[b * c:(b + 1) * c, :] = _rms(h_ref[b], gpre).astype(BF16)

    l0 = lbl_ref[0:1, :]
    l1 = lbl_ref[1:2, :]
    mx = jnp.maximum(l0, l1)
    e0 = jnp.exp(l0 - mx)
    e1 = jnp.exp(l1 - mx)
    lb_all = e0 / (e0 + e1)
    onorm = onorm_ref[...]

    def chunk_steps(head, b):
        lb = lb_all[:, head * HGRN_DK:(head + 1) * HGRN_DK]
        blk = p_ref[head % 2, b * c:(b + 1) * c, :]
        q = blk[:, 0:HGRN_DK]
        z = blk[:, HGRN_DK:2 * HGRN_DK]
        v = blk[:, 2 * HGRN_DK:3 * HGRN_DK]
        gg = blk[:, 3 * HGRN_DK:4 * HGRN_DK]
        gate = gg * jax.nn.sigmoid(gg)
        idx = head * BATCH + b
        for res in _hgrn_chunk(q, z, v, lb, 1.0 - lb, st_ref[idx], lv_ref[...], tril_ref[...]):
            if res is None:
                yield
        o, st_new = res
        st_ref[idx] = st_new
        og_ref[b * c:(b + 1) * c, head * HGRN_DK:(head + 1) * HGRN_DK] = (_rms(o, onorm) * gate).astype(BF16)

    p_ref[0] = _dot(xn_ref[...], w_ref[0])
    todo = [(head, b) for head in range(HGRN_HEADS) for b in range(BATCH)]
    live = []
    while todo or live:
        if todo:
            head, b = todo.pop(0)
            if b == 0 and head + 1 < HGRN_HEADS:
                p_ref[(head + 1) % 2] = _dot(xn_ref[...], w_ref[head + 1])
            live.append(chunk_steps(head, b))
        live = [g for g in live if next(g, StopIteration) is not StopIteration]

    y = _dot(og_ref[...], wout_ref[...])
    gpost = gpost_ref[...]
    for b in range(BATCH):
        o_ref[b] = h_ref[b] + _rms(y[b * c:(b + 1) * c, :], gpost)


def _hgrn(h, gpre, w_heads, lb_logits, onorm, wout, gpost):
    c = HGRN_CHUNK
    lv = jnp.asarray(_level_table(c))
    tril = jnp.asarray(np.tril(np.ones((c, c), np.float32)), dtype=BF16)
    blk = pl.BlockSpec((BATCH, c, D_MODEL), lambda j: (0, j, 0))
    return pl.pallas_call(
        _hgrn_kernel,
        out_shape=jax.ShapeDtypeStruct((BATCH, SEQ, D_MODEL), F32),
        grid=(SEQ // c,),
        in_specs=[blk, _resident((1, D_MODEL)), _resident((HGRN_HEADS, D_MODEL, 4 * HGRN_DK)),
                  _resident((2, D_MODEL)), _resident((1, HGRN_DK)), _resident((D_MODEL, D_MODEL)),
                  _resident((1, D_MODEL)), _resident((c, c)), _resident((c, c))],
        out_specs=blk,
        scratch_shapes=[pltpu.VMEM((BATCH * c, D_MODEL), BF16),
                        pltpu.VMEM((2, BATCH * c, 4 * HGRN_DK), F32),
                        pltpu.VMEM((BATCH * c, D_MODEL), BF16),
                        pltpu.VMEM((HGRN_HEADS * BATCH, HGRN_DK, HGRN_DK), F32)],
        compiler_params=pltpu.CompilerParams(
            dimension_semantics=("arbitrary",), vmem_limit_bytes=VMEM_LIMIT),
        name="hgrn",
    )(h, gpre, w_heads, lb_logits, onorm, wout, gpost, lv, tril)


def _ple_stage(x, p, gpre_ref, wg_ref, wp_ref, gpost_ref):
    xn = _rms(x, gpre_ref[...]).astype(BF16)
    gate = jax.nn.sigmoid(_dot(xn, wg_ref[...]))
    pp = _dot(p.astype(BF16), wp_ref[...])
    return x + _rms(gate * pp, gpost_ref[...])


def _ple_specs():
    return [_resident((1, D_MODEL)), _resident((D_MODEL, D_MODEL)),
            _resident((PLE_DIM, D_MODEL)), _resident((1, D_MODEL))]


def _ffn_ple_kv_kernel(h_ref, p_ref, f_gpre, f_win, f_wout, f_gpost, gpre_ref, wg_ref, wp_ref, gpost_ref,
                       kvn_ref, wk_ref, wvt_ref, wf_ref, bf_ref, tril_ref, ek_ref, eq_ref,
                       o_ref, k_ref, kx_ref, qx_ref, vt_ref, act_ref, carry_ref):
    hn = _ffn_stage(h_ref[0], f_gpre, f_win, f_wout, f_gpost, act_ref)
    hn = _ple_stage(hn, p_ref[0], gpre_ref, wg_ref, wp_ref, gpost_ref)
    o_ref[0] = hn
    xn = _rms(hn, kvn_ref[...]).astype(BF16)
    k_ref[0] = _dot(xn, wk_ref[...]).astype(BF16)
    vt = _dot_nt(wvt_ref[...], xn).astype(BF16)
    ones = jnp.ones((VT_ROWS - FOX_HEAD_DIM, ROW_TILE), BF16)
    for hd in range(FOX_HEADS):
        vt_ref[0, 0, hd, 0:FOX_HEAD_DIM, :] = vt[hd * FOX_HEAD_DIM:(hd + 1) * FOX_HEAD_DIM, :]
        vt_ref[0, 0, hd, FOX_HEAD_DIM:VT_ROWS, :] = ones

    @pl.when(pl.program_id(1) == 0)
    def _():
        carry_ref[...] = jnp.zeros_like(carry_ref)

    x = _dot(xn, wf_ref[...]) + bf_ref[...]
    ls = jnp.minimum(x, 0.0) - jnp.log1p(jnp.exp(-jnp.abs(x)))
    h1, h2, h3 = _split3(ls)
    tril = tril_ref[...]
    cum = _dot(tril, h1) + _dot(tril, h2) + _dot(tril, h3) + carry_ref[0:1, :]
    carry_ref[...] = jnp.broadcast_to(cum[ROW_TILE - 1:ROW_TILE, :], carry_ref.shape)
    c2 = cum * LOG2E
    hi = c2.astype(BF16).astype(F32)
    r1 = c2 - hi
    mid = r1.astype(BF16).astype(F32)
    lo = r1 - mid
    lane = lax.broadcasted_iota(jnp.int32, c2.shape, 1)
    cx = jnp.where(lane < FOX_HEADS, hi, jnp.where(lane < 2 * FOX_HEADS, mid, jnp.where(
        lane < 3 * FOX_HEADS, lo, jnp.where(lane == 3 * FOX_HEADS, 1.0, 0.0)))).astype(BF16)
    kx_ref[0] = _dot(cx, ek_ref[...]).astype(BF16)
    qx_ref[0] = _dot(cx, eq_ref[...]).astype(BF16)


def _bias_placement():
    ek = np.zeros((LANES, D_MODEL), np.float32)
    eq = np.zeros((LANES, D_MODEL), np.float32)
    one_lane = 3 * FOX_HEADS
    for head in range(FOX_HEADS):
        base = (head // 2) * LANES + (head % 2) * BIAS_LANES
        for i in range(3):
            ek[i * FOX_HEADS + head, base + i] = 1.0
            eq[one_lane, base + i] = -1.0
            ek[one_lane, base + 3 + i] = 1.0
            eq[i * FOX_HEADS + head, base + 3 + i] = 1.0
    return jnp.asarray(ek, dtype=BF16), jnp.asarray(eq, dtype=BF16)


def _ffn_ple_kv(h, p, ffn, ple, kvn, wk, wvt, wf3, bf3):
    assert ROW_TILE == ATT_BLOCK
    nblk = SEQ // ROW_TILE
    hs = pl.BlockSpec((1, ROW_TILE, D_MODEL), lambda b, i: (b, i, 0))
    ps = pl.BlockSpec((1, ROW_TILE, PLE_DIM), lambda b, i: (b, i, 0))
    vts = pl.BlockSpec((1, 1, FOX_HEADS, VT_ROWS, ROW_TILE), lambda b, i: (b, i, 0, 0, 0))
    tril = jnp.asarray(np.tril(np.ones((ROW_TILE, ROW_TILE), np.float32)), dtype=BF16)
    ek, eq = _bias_placement()
    return pl.pallas_call(
        _ffn_ple_kv_kernel,
        out_shape=(jax.ShapeDtypeStruct((BATCH, SEQ, D_MODEL), F32),
                   jax.ShapeDtypeStruct((BATCH, SEQ, D_MODEL), BF16),
                   jax.ShapeDtypeStruct((BATCH, SEQ, D_MODEL), BF16),
                   jax.ShapeDtypeStruct((BATCH, SEQ, D_MODEL), BF16),
                   jax.ShapeDtypeStruct((BATCH, nblk, FOX_HEADS, VT_ROWS, ROW_TILE), BF16)),
        grid=(BATCH, nblk),
        in_specs=[hs, ps] + _ffn_specs(ffn[0]) + _ple_specs() + [
            _resident((1, D_MODEL)), _resident((D_MODEL, D_MODEL)), _resident((D_MODEL, D_MODEL)),
            _resident((D_MODEL, LANES)), _resident((1, LANES)),
            _resident((ROW_TILE, ROW_TILE)), _resident((LANES, D_MODEL)), _resident((LANES, D_MODEL))],
        out_specs=(hs, hs, hs, hs, vts),
        scratch_shapes=[_ACT_SCRATCH, pltpu.VMEM((SUBLANES, LANES), F32)],
        compiler_params=pltpu.CompilerParams(
            dimension_semantics=("arbitrary", "arbitrary"), vmem_limit_bytes=VMEM_LIMIT),
        name="ffn_ple_kv",
    )(h, p, *ffn[1], *ple, kvn, wk, wvt, wf3, bf3, tril, ek, eq)


def _ffn_qg_kernel(h_ref, f_gpre, f_win, f_wout, f_gpost, gpre_ref, w_ref, o_ref, q_ref, g_ref, act_ref):
    hn = _ffn_stage(h_ref[...], f_gpre, f_win, f_wout, f_gpost, act_ref)
    o_ref[...] = hn
    xn = _rms(hn, gpre_ref[...]).astype(BF16)
    qg = _dot(xn, w_ref[...])
    q_ref[...] = (qg[:, :D_MODEL] * (FOX_HEAD_DIM ** -0.5 * LOG2E)).astype(BF16)
    g_ref[...] = jax.nn.sigmoid(qg[:, D_MODEL:])


def _ffn_qg(h2d, ffn, gpre, w):
    rows = h2d.shape[0]
    row_spec = pl.BlockSpec((ROW_TILE, D_MODEL), lambda i: (i, 0))
    return pl.pallas_call(
        _ffn_qg_kernel,
        out_shape=(jax.ShapeDtypeStruct((rows, D_MODEL), F32),
                   jax.ShapeDtypeStruct((rows, D_MODEL), BF16),
                   jax.ShapeDtypeStruct((rows, D_MODEL), F32)),
        grid=(rows // ROW_TILE,),
        in_specs=[row_spec] + _ffn_specs(ffn[0]) + [_resident((1, D_MODEL)), _resident((D_MODEL, 2 * D_MODEL))],
        out_specs=(row_spec, row_spec, row_spec),
        scratch_shapes=[_ACT_SCRATCH],
        compiler_params=pltpu.CompilerParams(
            dimension_semantics=("arbitrary",), vmem_limit_bytes=VMEM_LIMIT),
        name="ffn_qg",
    )(h2d, *ffn[1], gpre, w)


def _attn_schedule():
    t = ATT_BLOCK
    items = []
    for qi in range(SEQ // t):
        if qi % 2 == 0:
            blk = [(qi * t, t, 0)]
        else:
            blk = [((qi - 1) * t, 2 * t, t)]
        blk += [(2 * j * t, 2 * t, None) for j in range(qi // 2)]
        for n, (k0, nk, off) in enumerate(blk):
            items.append((qi, k0, nk, off, n == 0, n == len(blk) - 1))
    return items


def _attn_kernel(q_ref, qx_ref, k_ref, kx_ref, vt_ref, g_ref, o_ref, s0_ref, s1_ref, qcat_ref):
    t = ATT_BLOCK
    s_refs = (s0_ref, s1_ref)
    lane = lax.broadcasted_iota(jnp.int32, (t, LANES), 1)

    def load_queries(qi):
        q2 = q_ref[0, qi * t:(qi + 1) * t, :].astype(F32)
        qx2 = qx_ref[0, qi * t:(qi + 1) * t, :].astype(F32)
        for hd in range(2):
            in_head = (lane >= hd * FOX_HEAD_DIM) & (lane < (hd + 1) * FOX_HEAD_DIM)
            in_bias = (lane >= hd * BIAS_LANES) & (lane < (hd + 1) * BIAS_LANES)
            qcat_ref[hd, :, 0:LANES] = jnp.where(in_head, q2, 0.0).astype(BF16)
            qcat_ref[hd, :, LANES:2 * LANES] = jnp.where(in_bias, qx2, 0.0).astype(BF16)

    cq = ATT_QCHUNK
    nc = t // cq

    def visible(item, c):
        _, _, nk, off, _, _ = item
        return nk if off is None else min(nk, off + (c + 1) * cq)

    def produce(hd, item):
        _, k0, _, off, _, _ = item
        out = [None] * nc

        def chunk(c):
            nk = visible(item, c)
            kcat = jnp.concatenate([k_ref[0, k0:k0 + nk, :], kx_ref[0, k0:k0 + nk, :]], axis=1)
            st = _dot_nt(kcat, qcat_ref[hd, c * cq:(c + 1) * cq, :])
            if off is not None:
                key = lax.broadcasted_iota(jnp.int32, (nk, cq), 0)
                qry = lax.broadcasted_iota(jnp.int32, (nk, cq), 1) + c * cq
                st = jnp.where(key <= qry + off, st, -1e30)
            s_refs[hd][0:nk, c * cq:(c + 1) * cq] = st
            out[c] = jnp.max(st, axis=0, keepdims=True)

        return [functools.partial(chunk, c) for c in range(nc)], lambda: tuple(out)

    def consume(hd, item, state, mb):
        k0 = item[1]
        out = [None] * nc

        def chunk(c):
            nk = visible(item, c)
            m, acc = state[c]
            m_new = jnp.maximum(m, mb[c])
            p = jnp.exp2((s_refs[hd][0:nk, c * cq:(c + 1) * cq] - m_new).astype(BF16))
            vt = jnp.concatenate(
                [vt_ref[0, r // t, hd, :, r % t:r % t + cq] for r in range(k0, k0 + nk, cq)], axis=1)
            out[c] = (m_new, jnp.exp2(m - m_new) * acc + _dot(vt, p))

        return [functools.partial(chunk, c) for c in range(nc)], lambda: tuple(out)

    def interleave(a, b):
        for c in range(max(len(a), len(b))):
            if c < len(a):
                a[c]()
            if c < len(b):
                b[c]()

    def finalize(qi, states):
        for c in range(nc):
            ot = [st[c][1][0:FOX_HEAD_DIM, :] / st[c][1][FOX_HEAD_DIM:FOX_HEAD_DIM + 1, :] for st in states]
            o = jnp.concatenate(ot, axis=0).T
            r0 = qi * t + c * cq
            o_ref[0, r0:r0 + cq, :] = (o * g_ref[0, r0:r0 + cq, :]).astype(BF16)

    init = tuple((jnp.full((1, cq), -1e30, F32), jnp.zeros((VT_ROWS, cq), F32)) for _ in range(nc))

    items = _attn_schedule()
    st = [init, init]
    mb = [None, None]
    done0 = None
    for g, item in enumerate(items):
        qi, first = item[0], item[4]
        if first:
            load_queries(qi)
        p0, mb0 = produce(0, item)
        if g > 0:
            prev = items[g - 1]
            c1, st1 = consume(1, prev, st[1], mb[1])
            interleave(p0, c1)
            st[1] = st1()
            if prev[5]:
                finalize(prev[0], (done0, st[1]))
                st[1] = init
        else:
            interleave(p0, [])
        mb[0] = mb0()
        c0, st0 = consume(0, item, init if first else st[0], mb[0])
        p1, mb1 = produce(1, item)
        interleave(c0, p1)
        st[0] = st0()
        if item[5]:
            done0 = st[0]
        mb[1] = mb1()
    last = items[-1]
    c1, st1 = consume(1, last, st[1], mb[1])
    interleave(c1, [])
    finalize(last[0], (done0, st1()))


def _attn(q, qx, k, kx, vt, g):
    t = ATT_BLOCK
    nblk = SEQ // t
    ks = pl.BlockSpec((1, SEQ, LANES), lambda b, hp: (b, 0, hp))
    vts = pl.BlockSpec((1, nblk, 2, VT_ROWS, t), lambda b, hp: (b, 0, hp, 0, 0))
    return pl.pallas_call(
        _attn_kernel,
        out_shape=jax.ShapeDtypeStruct((BATCH, SEQ, D_MODEL), BF16),
        grid=(BATCH, FOX_PAIRS),
        in_specs=[ks, ks, ks, ks, vts, ks],
        out_specs=ks,
        scratch_shapes=[pltpu.VMEM((2 * t, t), F32), pltpu.VMEM((2 * t, t), F32),
                        pltpu.VMEM((2, t, 2 * LANES), BF16)],
        compiler_params=pltpu.CompilerParams(
            dimension_semantics=("arbitrary", "arbitrary"), vmem_limit_bytes=VMEM_LIMIT),
        name="fox_attn",
    )(q, qx, k, kx, vt, g)


def _out_ffn_ple_kernel(h_ref, a_ref, p_ref, w_ref, mpost_ref, f_gpre, f_win, f_wout, f_gpost,
                        gpre_ref, wg_ref, wp_ref, gpost_ref, o_ref, act_ref):
    hn = h_ref[...] + _rms(_dot(a_ref[...], w_ref[...]), mpost_ref[...])
    hn = _ffn_stage(hn, f_gpre, f_win, f_wout, f_gpost, act_ref)
    o_ref[...] = _ple_stage(hn, p_ref[...], gpre_ref, wg_ref, wp_ref, gpost_ref)


def _out_ffn_ple(h2d, a2d, p2d, p_row0, w, mpost, ffn, ple):
    rows = h2d.shape[0]
    row_spec = pl.BlockSpec((ROW_TILE, D_MODEL), lambda i: (i, 0))
    p_spec = pl.BlockSpec((ROW_TILE, PLE_DIM), lambda i: (p_row0 // ROW_TILE + i, 0))
    return pl.pallas_call(
        _out_ffn_ple_kernel,
        out_shape=jax.ShapeDtypeStruct((rows, D_MODEL), F32),
        grid=(rows // ROW_TILE,),
        in_specs=[row_spec, row_spec, p_spec, _resident((D_MODEL, D_MODEL)), _resident((1, D_MODEL))]
        + _ffn_specs(ffn[0]) + _ple_specs(),
        out_specs=row_spec,
        scratch_shapes=[_ACT_SCRATCH],
        compiler_params=pltpu.CompilerParams(
            dimension_semantics=("arbitrary",), vmem_limit_bytes=VMEM_LIMIT),
        name="out_ffn_ple",
    )(h2d, a2d, p2d, w, mpost, *ffn[1], *ple)


def kernel(x, p, ffn1_norm_pre, ffn1_w_in, ffn1_w_out, ffn1_norm_post, mix_norm_pre, mix_norm_post, ffn2_norm_pre, ffn2_w_in, ffn2_w_out, ffn2_norm_post, hgrn_w_in, hgrn_lb_logits, hgrn_out_norm, hgrn_w_out, kv_norm, fox_w_kvf, fox_b_f, fox_w_qg, fox_w_out, ple_norm_pre, ple_w_gate, ple_w_proj, ple_norm_post):
    assert x.shape == (BATCH, SEQ, D_MODEL) and p.shape == (2, BATCH, SEQ, PLE_DIM)
    rows = BATCH * SEQ
    bf = lambda w: w.astype(BF16)
    vec = lambda g: g.reshape(1, -1).astype(F32)
    flat = lambda h: h.reshape(rows, D_MODEL)
    cube = lambda h: h.reshape(BATCH, SEQ, D_MODEL)
    p2 = p.reshape(2 * BATCH, SEQ, PLE_DIM)

    f1_in, f1_out = _cast_bf16(ffn1_w_in), _cast_bf16(ffn1_w_out)
    f2_in, f2_out = _cast_bf16(ffn2_w_in), _cast_bf16(ffn2_w_out)

    def ffn1(i):
        return i, (vec(ffn1_norm_pre[i]), f1_in, f1_out, vec(ffn1_norm_post[i]))

    def ffn2(i):
        return i, (vec(ffn2_norm_pre[i]), f2_in, f2_out, vec(ffn2_norm_post[i]))

    def ple(i):
        return vec(ple_norm_pre[i]), bf(ple_w_gate[i]), bf(ple_w_proj[i]), vec(ple_norm_post[i])

    h = _ffn(flat(x), ffn1(0))
    w_heads = bf(hgrn_w_in[0].reshape(D_MODEL, 4, HGRN_HEADS, HGRN_DK).transpose(2, 0, 1, 3)
                 .reshape(HGRN_HEADS, D_MODEL, 4 * HGRN_DK))
    h = _hgrn(cube(h), vec(mix_norm_pre[0]), w_heads, hgrn_lb_logits.astype(F32),
              vec(hgrn_out_norm[0]), bf(hgrn_w_out[0]), vec(mix_norm_post[0]))
    pad = LANES - 3 * FOX_HEADS
    wf3 = jnp.pad(jnp.tile(fox_w_kvf[:, 2 * D_MODEL:], (1, 3)), ((0, 0), (0, pad)))
    bf3 = jnp.pad(jnp.tile(fox_b_f, 3), (0, pad)).reshape(1, LANES).astype(F32)
    h, k_sh, kx_sh, qx_sh, vt_sh = _ffn_ple_kv(
        h, p2, ffn2(0), ple(0),
        vec(kv_norm), bf(fox_w_kvf[:, :D_MODEL]), bf(fox_w_kvf[:, D_MODEL:2 * D_MODEL].T), bf(wf3), bf3)

    h, q, g = _ffn_qg(flat(h), ffn1(1), vec(mix_norm_pre[1]), bf(fox_w_qg[0]))
    a = _attn(cube(q), qx_sh, k_sh, kx_sh, vt_sh, cube(g))
    h = _out_ffn_ple(h, flat(a), p2.reshape(2 * rows, PLE_DIM), rows, bf(fox_w_out[0]),
                     vec(mix_norm_post[1]), ffn2(1), ple(1))
    return cube(h)
```

```python
import functools

import numpy as np
import jax
import jax.numpy as jnp
from jax import lax
from jax.experimental import pallas as pl
from jax.experimental.pallas import tpu as pltpu

D_MODEL = 1024
BATCH = 8
SEQ = 4096
D_FF = 2816
PLE_DIM = 256
NORM_EPS = 1e-6
HGRN_HEADS = 8
HGRN_DK = 128
FOX_HEADS = 16
FOX_HEAD_DIM = 64
FOX_PAIRS = FOX_HEADS // 2

SUBLANES = 8
LANES = 128
MXU_DIM = 256

ROW_TILE = 512
FF_CHUNK = MXU_DIM
FFN_ROW_PARTS = 2
HGRN_CHUNK = 128
ATT_BLOCK = 512
ATT_QCHUNK = MXU_DIM
BIAS_LANES = 8
VT_ROWS = FOX_HEAD_DIM + 16
LOG2E = 1.4426950408889634
VMEM_LIMIT = 56 * 1024 * 1024

BF16 = jnp.bfloat16
F32 = jnp.float32


def _rms(x, g):
    return x * lax.rsqrt(jnp.mean(x * x, axis=-1, keepdims=True) + NORM_EPS) * g


def _dot(a, b):
    return jnp.dot(a, b, preferred_element_type=F32)


def _dot_nt(a, b):
    return lax.dot_general(a, b, (((1,), (1,)), ((), ())), preferred_element_type=F32)


def _dot_tn(a, b):
    return lax.dot_general(a, b, (((0,), (0,)), ((), ())), preferred_element_type=F32)


def _resident(shape):
    zeros = (0,) * len(shape)
    return pl.BlockSpec(shape, lambda *_: zeros, pipeline_mode=pl.Buffered(1))


def _split3(x):
    h1 = x.astype(BF16)
    r1 = x - h1.astype(F32)
    h2 = r1.astype(BF16)
    h3 = (r1 - h2.astype(F32)).astype(BF16)
    return h1, h2, h3


def _ffn_stage(x, gpre_ref, win_ref, wout_ref, gpost_ref, act_ref):
    rows = x.shape[0]
    part = rows // FFN_ROW_PARTS

    def steps(r0):
        xh = x[r0:r0 + part, :]
        xn = _rms(xh, gpre_ref[...]).astype(BF16)
        yield None
        for j in range(D_FF // FF_CHUNK):
            lo = j * FF_CHUNK
            gate = _dot(xn, win_ref[:, lo:lo + FF_CHUNK])
            up = _dot(xn, win_ref[:, D_FF + lo:D_FF + lo + FF_CHUNK])
            act_ref[r0:r0 + part, lo:lo + FF_CHUNK] = (gate * jax.nn.sigmoid(gate) * up).astype(BF16)
            yield None
        y = _dot(act_ref[r0:r0 + part, :], wout_ref[...])
        yield None
        yield xh + 0.5 * _rms(y, gpost_ref[...])

    todo = [steps(r0) for r0 in range(0, rows, part)]
    live, outs = [], []
    while todo or live:
        if todo:
            live.append(todo.pop(0))
        nxt = []
        for g in live:
            res = next(g)
            if res is None:
                nxt.append(g)
            else:
                outs.append(res)
        live = nxt
    return jnp.concatenate(outs, axis=0)


def _layer_resident(shape, layer):
    return pl.BlockSpec((None,) + shape, lambda *_: (layer, 0, 0), pipeline_mode=pl.Buffered(1))


def _ffn_specs(layer):
    return [_resident((1, D_MODEL)), _layer_resident((D_MODEL, 2 * D_FF), layer),
            _layer_resident((D_FF, D_MODEL), layer), _resident((1, D_MODEL))]


def _cast_kernel(w_ref, o_ref):
    o_ref[...] = w_ref[...].astype(BF16)


def _cast_bf16(w):
    layers, rows, cols = w.shape
    rb = rows // 4
    spec = pl.BlockSpec((1, rb, cols), lambda l, i: (l, i, 0))
    return pl.pallas_call(
        _cast_kernel,
        out_shape=jax.ShapeDtypeStruct(w.shape, BF16),
        grid=(layers, rows // rb),
        in_specs=[spec],
        out_specs=spec,
        compiler_params=pltpu.CompilerParams(
            dimension_semantics=("arbitrary", "arbitrary"), vmem_limit_bytes=VMEM_LIMIT),
        name="cast_bf16",
    )(w)


_ACT_SCRATCH = pltpu.VMEM((ROW_TILE, D_FF), BF16)


def _ffn_kernel(x_ref, gpre_ref, win_ref, wout_ref, gpost_ref, o_ref, act_ref):
    o_ref[...] = _ffn_stage(x_ref[...], gpre_ref, win_ref, wout_ref, gpost_ref, act_ref)


def _ffn(h2d, ffn):
    rows = h2d.shape[0]
    row_spec = pl.BlockSpec((ROW_TILE, D_MODEL), lambda i: (i, 0))
    return pl.pallas_call(
        _ffn_kernel,
        out_shape=jax.ShapeDtypeStruct((rows, D_MODEL), F32),
        grid=(rows // ROW_TILE,),
        in_specs=[row_spec] + _ffn_specs(ffn[0]),
        out_specs=row_spec,
        scratch_shapes=[_ACT_SCRATCH],
        compiler_params=pltpu.CompilerParams(
            dimension_semantics=("arbitrary",), vmem_limit_bytes=VMEM_LIMIT),
        name="ffn",
    )(h2d, *ffn[1])


def _level_table(c):
    t = np.arange(c)[:, None]
    s = np.arange(c)[None, :]
    x = np.maximum(t ^ s, 1)
    lv = np.floor(np.log2(x)).astype(np.int32)
    lv = np.where(s == t, -1, lv)
    lv = np.where(s > t, -2, lv)
    return lv.astype(np.int32)


def _hgrn_chunk(q, z, v, lb, oml, st, lv, tril):
    c = HGRN_CHUNK
    nt = c // SUBLANES
    e = jnp.exp(-jnp.abs(z))
    r = 1.0 / (1.0 + e)
    er = e * r
    pos = z >= 0
    f = lb + oml * jnp.where(pos, r, er)
    kk = oml * jnp.where(pos, er, r)
    lf = jnp.log2(f)

    h1, h2, h3 = _split3(lf)
    cum = _dot(tril, h1) + _dot(tril, h2) + _dot(tril, h3)
    yield None

    sub = lax.broadcasted_iota(jnp.int32, (SUBLANES, LANES), 0)
    zero = jnp.zeros((SUBLANES, LANES), F32)
    tiles = [cum[SUBLANES * j:SUBLANES * (j + 1), :] for j in range(nt)]

    def row(j, rr):
        return jnp.broadcast_to(cum[SUBLANES * j + rr:SUBLANES * j + rr + 1, :], (SUBLANES, LANES))

    b1 = [row(j, 1) for j in range(nt)]
    b3 = [row(j, 3) for j in range(nt)]
    b5 = [row(j, 5) for j in range(nt)]
    b7 = [row(j, 7) for j in range(nt)]
    qt = [q[SUBLANES * j:SUBLANES * (j + 1), :] for j in range(nt)]
    kt = [kk[SUBLANES * j:SUBLANES * (j + 1), :] for j in range(nt)]

    def cat(ts):
        return jnp.concatenate(ts, axis=0).astype(BF16)

    v_bf = v.astype(BF16)
    kk_bf = kk.astype(BF16)
    sc = jnp.where(lv == -1, _dot_nt(q.astype(BF16), kk_bf), 0.0)
    sc = jnp.where(lv == 0, _dot_nt((q * f).astype(BF16), kk_bf), sc)
    yield None

    level = 1
    blk = 2
    while blk < c:
        qs, ks = [], []
        for j in range(nt):
            if blk < SUBLANES:
                ref = jnp.where(sub < 4, b1[j], b5[j]) if blk == 2 else b3[j]
                e = jnp.exp2(-jnp.abs(tiles[j] - ref))
                qs.append(qt[j] * e)
                ks.append(kt[j] * e)
            else:
                m = blk // SUBLANES
                b = j // m
                if b % 2 == 1:
                    qs.append(qt[j] * jnp.exp2(tiles[j] - b7[b * m - 1]))
                    ks.append(zero)
                else:
                    qs.append(zero)
                    ks.append(kt[j] * jnp.exp2(b7[b * m + m - 1] - tiles[j]))
        sc = jnp.where(lv == level, _dot_nt(cat(qs), cat(ks)), sc)
        if blk < c // 2:
            yield None
        level += 1
        blk *= 2

    last = b7[nt - 1]
    qe = cat([qt[j] * jnp.exp2(tiles[j]) for j in range(nt)])
    ke = cat([kt[j] * jnp.exp2(last - tiles[j]) for j in range(nt)])
    yield None
    o = _dot_nt(qe, st.astype(BF16)) + _dot(sc.astype(BF16), v_bf)
    st_new = st * jnp.exp2(last[0:1, :]) + _dot_tn(v_bf, ke)
    yield o, st_new


def _hgrn_kernel(h_ref, gpre_ref, w_ref, lbl_ref, onorm_ref, wout_ref, gpost_ref, lv_ref, tril_ref,
                 o_ref, xn_ref, p_ref, og_ref, st_ref):
    c = HGRN_CHUNK

    @pl.when(pl.program_id(0) == 0)
    def _():
        st_ref[...] = jnp.zeros_like(st_ref)

    gpre = gpre_ref[...]
    for b in range(BATCH):
        xn_ref[b * c:(b + 1) * c, :] = _rms(h_ref[b], gpre).astype(BF16)

    l0 = lbl_ref[0:1, :]
    l1 = lbl_ref[1:2, :]
    mx = jnp.maximum(l0, l1)
    e0 = jnp.exp(l0 - mx)
    e1 = jnp.exp(l1 - mx)
    lb_all = e0 / (e0 + e1)
    onorm = onorm_ref[...]

    def chunk_steps(head, b):
        lb = lb_all[:, head * HGRN_DK:(head + 1) * HGRN_DK]
        blk = p_ref[head % 2, b * c:(b + 1) * c, :]
        q = blk[:, 0:HGRN_DK]
        z = blk[:, HGRN_DK:2 * HGRN_DK]
        v = blk[:, 2 * HGRN_DK:3 * HGRN_DK]
        gg = blk[:, 3 * HGRN_DK:4 * HGRN_DK]
        gate = gg * jax.nn.sigmoid(gg)
        idx = head * BATCH + b
        for res in _hgrn_chunk(q, z, v, lb, 1.0 - lb, st_ref[idx], lv_ref[...], tril_ref[...]):
            if res is None:
                yield
        o, st_new = res
        st_ref[idx] = st_new
        og_ref[b * c:(b + 1) * c, head * HGRN_DK:(head + 1) * HGRN_DK] = (_rms(o, onorm) * gate).astype(BF16)

    p_ref[0] = _dot(xn_ref[...], w_ref[0])
    todo = [(head, b) for head in range(HGRN_HEADS) for b in range(BATCH)]
    live = []
    while todo or live:
        if todo:
            head, b = todo.pop(0)
            if b == 0 and head + 1 < HGRN_HEADS:
                p_ref[(head + 1) % 2] = _dot(xn_ref[...], w_ref[head + 1])
            live.append(chunk_steps(head, b))
        live = [g for g in live if next(g, StopIteration) is not StopIteration]

    y = _dot(og_ref[...], wout_ref[...])
    gpost = gpost_ref[...]
    for b in range(BATCH):
        o_ref[b] = h_ref[b] + _rms(y[b * c:(b + 1) * c, :], gpost)


def _hgrn(h, gpre, w_heads, lb_logits, onorm, wout, gpost):
    c = HGRN_CHUNK
    lv = jnp.asarray(_level_table(c))
    tril = jnp.asarray(np.tril(np.ones((c, c), np.float32)), dtype=BF16)
    blk = pl.BlockSpec((BATCH, c, D_MODEL), lambda j: (0, j, 0))
    return pl.pallas_call(
        _hgrn_kernel,
        out_shape=jax.ShapeDtypeStruct((BATCH, SEQ, D_MODEL), F32),
        grid=(SEQ // c,),
        in_specs=[blk, _resident((1, D_MODEL)), _resident((HGRN_HEADS, D_MODEL, 4 * HGRN_DK)),
                  _resident((2, D_MODEL)), _resident((1, HGRN_DK)), _resident((D_MODEL, D_MODEL)),
                  _resident((1, D_MODEL)), _resident((c, c)), _resident((c, c))],
        out_specs=blk,
        scratch_shapes=[pltpu.VMEM((BATCH * c, D_MODEL), BF16),
                        pltpu.VMEM((2, BATCH * c, 4 * HGRN_DK), F32),
                        pltpu.VMEM((BATCH * c, D_MODEL), BF16),
                        pltpu.VMEM((HGRN_HEADS * BATCH, HGRN_DK, HGRN_DK), F32)],
        compiler_params=pltpu.CompilerParams(
            dimension_semantics=("arbitrary",), vmem_limit_bytes=VMEM_LIMIT),
        name="hgrn",
    )(h, gpre, w_heads, lb_logits, onorm, wout, gpost, lv, tril)


def _ple_stage(x, p, gpre_ref, wg_ref, wp_ref, gpost_ref):
    xn = _rms(x, gpre_ref[...]).astype(BF16)
    gate = jax.nn.sigmoid(_dot(xn, wg_ref[...]))
    pp = _dot(p.astype(BF16), wp_ref[...])
    return x + _rms(gate * pp, gpost_ref[...])


def _ple_specs():
    return [_resident((1, D_MODEL)), _resident((D_MODEL, D_MODEL)),
            _resident((PLE_DIM, D_MODEL)), _resident((1, D_MODEL))]


def _ffn_ple_kv_kernel(h_ref, p_ref, f_gpre, f_win, f_wout, f_gpost, gpre_ref, wg_ref, wp_ref, gpost_ref,
                       kvn_ref, wk_ref, wvt_ref, wf_ref, bf_ref, tril_ref, ek_ref, eq_ref,
                       o_ref, k_ref, kx_ref, qx_ref, vt_ref, act_ref, carry_ref):
    hn = _ffn_stage(h_ref[0], f_gpre, f_win, f_wout, f_gpost, act_ref)
    hn = _ple_stage(hn, p_ref[0], gpre_ref, wg_ref, wp_ref, gpost_ref)
    o_ref[0] = hn
    xn = _rms(hn, kvn_ref[...]).astype(BF16)
    k_ref[0] = _dot(xn, wk_ref[...]).astype(BF16)
    vt = _dot_nt(wvt_ref[...], xn).astype(BF16)
    ones = jnp.ones((VT_ROWS - FOX_HEAD_DIM, ROW_TILE), BF16)
    for hd in range(FOX_HEADS):
        vt_ref[0, 0, hd, 0:FOX_HEAD_DIM, :] = vt[hd * FOX_HEAD_DIM:(hd + 1) * FOX_HEAD_DIM, :]
        vt_ref[0, 0, hd, FOX_HEAD_DIM:VT_ROWS, :] = ones

    @pl.when(pl.program_id(1) == 0)
    def _():
        carry_ref[...] = jnp.zeros_like(carry_ref)

    x = _dot(xn, wf_ref[...]) + bf_ref[...]
    ls = jnp.minimum(x, 0.0) - jnp.log1p(jnp.exp(-jnp.abs(x)))
    h1, h2, h3 = _split3(ls)
    tril = tril_ref[...]
    cum = _dot(tril, h1) + _dot(tril, h2) + _dot(tril, h3) + carry_ref[0:1, :]
    carry_ref[...] = jnp.broadcast_to(cum[ROW_TILE - 1:ROW_TILE, :], carry_ref.shape)
    c2 = cum * LOG2E
    hi = c2.astype(BF16).astype(F32)
    r1 = c2 - hi
    mid = r1.astype(BF16).astype(F32)
    lo = r1 - mid
    lane = lax.broadcasted_iota(jnp.int32, c2.shape, 1)
    cx = jnp.where(lane < FOX_HEADS, hi, jnp.where(lane < 2 * FOX_HEADS, mid, jnp.where(
        lane < 3 * FOX_HEADS, lo, jnp.where(lane == 3 * FOX_HEADS, 1.0, 0.0)))).astype(BF16)
    kx_ref[0] = _dot(cx, ek_ref[...]).astype(BF16)
    qx_ref[0] = _dot(cx, eq_ref[...]).astype(BF16)


def _bias_placement():
    ek = np.zeros((LANES, D_MODEL), np.float32)
    eq = np.zeros((LANES, D_MODEL), np.float32)
    one_lane = 3 * FOX_HEADS
    for head in range(FOX_HEADS):
        base = (head // 2) * LANES + (head % 2) * BIAS_LANES
        for i in range(3):
            ek[i * FOX_HEADS + head, base + i] = 1.0
            eq[one_lane, base + i] = -1.0
            ek[one_lane, base + 3 + i] = 1.0
            eq[i * FOX_HEADS + head, base + 3 + i] = 1.0
    return jnp.asarray(ek, dtype=BF16), jnp.asarray(eq, dtype=BF16)


def _ffn_ple_kv(h, p, ffn, ple, kvn, wk, wvt, wf3, bf3):
    assert ROW_TILE == ATT_BLOCK
    nblk = SEQ // ROW_TILE
    hs = pl.BlockSpec((1, ROW_TILE, D_MODEL), lambda b, i: (b, i, 0))
    ps = pl.BlockSpec((1, ROW_TILE, PLE_DIM), lambda b, i: (b, i, 0))
    vts = pl.BlockSpec((1, 1, FOX_HEADS, VT_ROWS, ROW_TILE), lambda b, i: (b, i, 0, 0, 0))
    tril = jnp.asarray(np.tril(np.ones((ROW_TILE, ROW_TILE), np.float32)), dtype=BF16)
    ek, eq = _bias_placement()
    return pl.pallas_call(
        _ffn_ple_kv_kernel,
        out_shape=(jax.ShapeDtypeStruct((BATCH, SEQ, D_MODEL), F32),
                   jax.ShapeDtypeStruct((BATCH, SEQ, D_MODEL), BF16),
                   jax.ShapeDtypeStruct((BATCH, SEQ, D_MODEL), BF16),
                   jax.ShapeDtypeStruct((BATCH, SEQ, D_MODEL), BF16),
                   jax.ShapeDtypeStruct((BATCH, nblk, FOX_HEADS, VT_ROWS, ROW_TILE), BF16)),
        grid=(BATCH, nblk),
        in_specs=[hs, ps] + _ffn_specs(ffn[0]) + _ple_specs() + [
            _resident((1, D_MODEL)), _resident((D_MODEL, D_MODEL)), _resident((D_MODEL, D_MODEL)),
            _resident((D_MODEL, LANES)), _resident((1, LANES)),
            _resident((ROW_TILE, ROW_TILE)), _resident((LANES, D_MODEL)), _resident((LANES, D_MODEL))],
        out_specs=(hs, hs, hs, hs, vts),
        scratch_shapes=[_ACT_SCRATCH, pltpu.VMEM((SUBLANES, LANES), F32)],
        compiler_params=pltpu.CompilerParams(
            dimension_semantics=("arbitrary", "arbitrary"), vmem_limit_bytes=VMEM_LIMIT),
        name="ffn_ple_kv",
    )(h, p, *ffn[1], *ple, kvn, wk, wvt, wf3, bf3, tril, ek, eq)


def _ffn_qg_kernel(h_ref, f_gpre, f_win, f_wout, f_gpost, gpre_ref, w_ref, o_ref, q_ref, g_ref, act_ref):
    hn = _ffn_stage(h_ref[...], f_gpre, f_win, f_wout, f_gpost, act_ref)
    o_ref[...] = hn
    xn = _rms(hn, gpre_ref[...]).astype(BF16)
    qg = _dot(xn, w_ref[...])
    q_ref[...] = (qg[:, :D_MODEL] * (FOX_HEAD_DIM ** -0.5 * LOG2E)).astype(BF16)
    g_ref[...] = jax.nn.sigmoid(qg[:, D_MODEL:])


def _ffn_qg(h2d, ffn, gpre, w):
    rows = h2d.shape[0]
    row_spec = pl.BlockSpec((ROW_TILE, D_MODEL), lambda i: (i, 0))
    return pl.pallas_call(
        _ffn_qg_kernel,
        out_shape=(jax.ShapeDtypeStruct((rows, D_MODEL), F32),
                   jax.ShapeDtypeStruct((rows, D_MODEL), BF16),
                   jax.ShapeDtypeStruct((rows, D_MODEL), F32)),
        grid=(rows // ROW_TILE,),
        in_specs=[row_spec] + _ffn_specs(ffn[0]) + [_resident((1, D_MODEL)), _resident((D_MODEL, 2 * D_MODEL))],
        out_specs=(row_spec, row_spec, row_spec),
        scratch_shapes=[_ACT_SCRATCH],
        compiler_params=pltpu.CompilerParams(
            dimension_semantics=("arbitrary",), vmem_limit_bytes=VMEM_LIMIT),
        name="ffn_qg",
    )(h2d, *ffn[1], gpre, w)


def _attn_schedule():
    t = ATT_BLOCK
    items = []
    for qi in range(SEQ // t):
        if qi % 2 == 0:
            blk = [(qi * t, t, 0)]
        else:
            blk = [((qi - 1) * t, 2 * t, t)]
        blk += [(2 * j * t, 2 * t, None) for j in range(qi // 2)]
        for n, (k0, nk, off) in enumerate(blk):
            items.append((qi, k0, nk, off, n == 0, n == len(blk) - 1))
    return items


def _attn_kernel(q_ref, qx_ref, k_ref, kx_ref, vt_ref, g_ref, o_ref, s0_ref, s1_ref, qcat_ref):
    t = ATT_BLOCK
    s_refs = (s0_ref, s1_ref)
    lane = lax.broadcasted_iota(jnp.int32, (t, LANES), 1)

    def load_queries(qi):
        q2 = q_ref[0, qi * t:(qi + 1) * t, :].astype(F32)
        qx2 = qx_ref[0, qi * t:(qi + 1) * t, :].astype(F32)
        for hd in range(2):
            in_head = (lane >= hd * FOX_HEAD_DIM) & (lane < (hd + 1) * FOX_HEAD_DIM)
            in_bias = (lane >= hd * BIAS_LANES) & (lane < (hd + 1) * BIAS_LANES)
            qcat_ref[hd, :, 0:LANES] = jnp.where(in_head, q2, 0.0).astype(BF16)
            qcat_ref[hd, :, LANES:2 * LANES] = jnp.where(in_bias, qx2, 0.0).astype(BF16)

    cq = ATT_QCHUNK
    nc = t // cq

    def visible(item, c):
        _, _, nk, off, _, _ = item
        return nk if off is None else min(nk, off + (c + 1) * cq)

    def produce(hd, item):
        _, k0, _, off, _, _ = item
        out = [None] * nc

        def chunk(c):
            nk = visible(item, c)
            kcat = jnp.concatenate([k_ref[0, k0:k0 + nk, :], kx_ref[0, k0:k0 + nk, :]], axis=1)
            st = _dot_nt(kcat, qcat_ref[hd, c * cq:(c + 1) * cq, :])
            if off is not None:
                key = lax.broadcasted_iota(jnp.int32, (nk, cq), 0)
                qry = lax.broadcasted_iota(jnp.int32, (nk, cq), 1) + c * cq
                st = jnp.where(key <= qry + off, st, -1e30)
            s_refs[hd][0:nk, c * cq:(c + 1) * cq] = st
            out[c] = jnp.max(st, axis=0, keepdims=True)

        return [functools.partial(chunk, c) for c in range(nc)], lambda: tuple(out)

    def consume(hd, item, state, mb):
        k0 = item[1]
        out = [None] * nc

        def chunk(c):
            nk = visible(item, c)
            m, acc = state[c]
            m_new = jnp.maximum(m, mb[c])
            p = jnp.exp2((s_refs[hd][0:nk, c * cq:(c + 1) * cq] - m_new).astype(BF16))
            vt = jnp.concatenate(
                [vt_ref[0, r // t, hd, :, r % t:r % t + cq] for r in range(k0, k0 + nk, cq)], axis=1)
            out[c] = (m_new, jnp.exp2(m - m_new) * acc + _dot(vt, p))

        return [functools.partial(chunk, c) for c in range(nc)], lambda: tuple(out)

    def interleave(a, b):
        for c in range(max(len(a), len(b))):
            if c < len(a):
                a[c]()
            if c < len(b):
                b[c]()

    def finalize(qi, states):
        for c in range(nc):
            ot = [st[c][1][0:FOX_HEAD_DIM, :] / st[c][1][FOX_HEAD_DIM:FOX_HEAD_DIM + 1, :] for st in states]
            o = jnp.concatenate(ot, axis=0).T
            r0 = qi * t + c * cq
            o_ref[0, r0:r0 + cq, :] = (o * g_ref[0, r0:r0 + cq, :]).astype(BF16)

    init = tuple((jnp.full((1, cq), -1e30, F32), jnp.zeros((VT_ROWS, cq), F32)) for _ in range(nc))

    items = _attn_schedule()
    st = [init, init]
    mb = [None, None]
    done0 = None
    for g, item in enumerate(items):
        qi, first = item[0], item[4]
        if first:
            load_queries(qi)
        p0, mb0 = produce(0, item)
        if g > 0:
            prev = items[g - 1]
            c1, st1 = consume(1, prev, st[1], mb[1])
            interleave(p0, c1)
            st[1] = st1()
            if prev[5]:
                finalize(prev[0], (done0, st[1]))
                st[1] = init
        else:
            interleave(p0, [])
        mb[0] = mb0()
        c0, st0 = consume(0, item, init if first else st[0], mb[0])
        p1, mb1 = produce(1, item)
        interleave(c0, p1)
        st[0] = st0()
        if item[5]:
            done0 = st[0]
        mb[1] = mb1()
    last = items[-1]
    c1, st1 = consume(1, last, st[1], mb[1])
    interleave(c1, [])
    finalize(last[0], (done0, st1()))


def _attn(q, qx, k, kx, vt, g):
    t = ATT_BLOCK
    nblk = SEQ // t
    ks = pl.BlockSpec((1, SEQ, LANES), lambda b, hp: (b, 0, hp))
    vts = pl.BlockSpec((1, nblk, 2, VT_ROWS, t), lambda b, hp: (b, 0, hp, 0, 0))
    return pl.pallas_call(
        _attn_kernel,
        out_shape=jax.ShapeDtypeStruct((BATCH, SEQ, D_MODEL), BF16),
        grid=(BATCH, FOX_PAIRS),
        in_specs=[ks, ks, ks, ks, vts, ks],
        out_specs=ks,
        scratch_shapes=[pltpu.VMEM((2 * t, t), F32), pltpu.VMEM((2 * t, t), F32),
                        pltpu.VMEM((2, t, 2 * LANES), BF16)],
        compiler_params=pltpu.CompilerParams(
            dimension_semantics=("arbitrary", "arbitrary"), vmem_limit_bytes=VMEM_LIMIT),
        name="fox_attn",
    )(q, qx, k, kx, vt, g)


def _out_ffn_ple_kernel(h_ref, a_ref, p_ref, w_ref, mpost_ref, f_gpre, f_win, f_wout, f_gpost,
                        gpre_ref, wg_ref, wp_ref, gpost_ref, o_ref, act_ref):
    hn = h_ref[...] + _rms(_dot(a_ref[...], w_ref[...]), mpost_ref[...])
    hn = _ffn_stage(hn, f_gpre, f_win, f_wout, f_gpost, act_ref)
    o_ref[...] = _ple_stage(hn, p_ref[...], gpre_ref, wg_ref, wp_ref, gpost_ref)


def _out_ffn_ple(h2d, a2d, p2d, p_row0, w, mpost, ffn, ple):
    rows = h2d.shape[0]
    row_spec = pl.BlockSpec((ROW_TILE, D_MODEL), lambda i: (i, 0))
    p_spec = pl.BlockSpec((ROW_TILE, PLE_DIM), lambda i: (p_row0 // ROW_TILE + i, 0))
    return pl.pallas_call(
        _out_ffn_ple_kernel,
        out_shape=jax.ShapeDtypeStruct((rows, D_MODEL), F32),
        grid=(rows // ROW_TILE,),
        in_specs=[row_spec, row_spec, p_spec, _resident((D_MODEL, D_MODEL)), _resident((1, D_MODEL))]
        + _ffn_specs(ffn[0]) + _ple_specs(),
        out_specs=row_spec,
        scratch_shapes=[_ACT_SCRATCH],
        compiler_params=pltpu.CompilerParams(
            dimension_semantics=("arbitrary",), vmem_limit_bytes=VMEM_LIMIT),
        name="out_ffn_ple",
    )(h2d, a2d, p2d, w, mpost, *ffn[1], *ple)


def kernel(x, p, ffn1_norm_pre, ffn1_w_in, ffn1_w_out, ffn1_norm_post, mix_norm_pre, mix_norm_post, ffn2_norm_pre, ffn2_w_in, ffn2_w_out, ffn2_norm_post, hgrn_w_in, hgrn_lb_logits, hgrn_out_norm, hgrn_w_out, kv_norm, fox_w_kvf, fox_b_f, fox_w_qg, fox_w_out, ple_norm_pre, ple_w_gate, ple_w_proj, ple_norm_post):
    assert x.shape == (BATCH, SEQ, D_MODEL) and p.shape == (2, BATCH, SEQ, PLE_DIM)
    rows = BATCH * SEQ
    bf = lambda w: w.astype(BF16)
    vec = lambda g: g.reshape(1, -1).astype(F32)
    flat = lambda h: h.reshape(rows, D_MODEL)
    cube = lambda h: h.reshape(BATCH, SEQ, D_MODEL)
    p2 = p.reshape(2 * BATCH, SEQ, PLE_DIM)

    f1_in, f1_out = _cast_bf16(ffn1_w_in), _cast_bf16(ffn1_w_out)
    f2_in, f2_out = _cast_bf16(ffn2_w_in), _cast_bf16(ffn2_w_out)

    def ffn1(i):
        return i, (vec(ffn1_norm_pre[i]), f1_in, f1_out, vec(ffn1_norm_post[i]))

    def ffn2(i):
        return i, (vec(ffn2_norm_pre[i]), f2_in, f2_out, vec(ffn2_norm_post[i]))

    def ple(i):
        return vec(ple_norm_pre[i]), bf(ple_w_gate[i]), bf(ple_w_proj[i]), vec(ple_norm_post[i])

    h = _ffn(flat(x), ffn1(0))
    w_heads = bf(hgrn_w_in[0].reshape(D_MODEL, 4, HGRN_HEADS, HGRN_DK).transpose(2, 0, 1, 3)
                 .reshape(HGRN_HEADS, D_MODEL, 4 * HGRN_DK))
    h = _hgrn(cube(h), vec(mix_norm_pre[0]), w_heads, hgrn_lb_logits.astype(F32),
              vec(hgrn_out_norm[0]), bf(hgrn_w_out[0]), vec(mix_norm_post[0]))
    pad = LANES - 3 * FOX_HEADS
    wf3 = jnp.pad(jnp.tile(fox_w_kvf[:, 2 * D_MODEL:], (1, 3)), ((0, 0), (0, pad)))
    bf3 = jnp.pad(jnp.tile(fox_b_f, 3), (0, pad)).reshape(1, LANES).astype(F32)
    h, k_sh, kx_sh, qx_sh, vt_sh = _ffn_ple_kv(
        h, p2, ffn2(0), ple(0),
        vec(kv_norm), bf(fox_w_kvf[:, :D_MODEL]), bf(fox_w_kvf[:, D_MODEL:2 * D_MODEL].T), bf(wf3), bf3)

    h, q, g = _ffn_qg(flat(h), ffn1(1), vec(mix_norm_pre[1]), bf(fox_w_qg[0]))
    a = _attn(cube(q), qx_sh, k_sh, kx_sh, vt_sh, cube(g))
    h = _out_ffn_ple(h, flat(a), p2.reshape(2 * rows, PLE_DIM), rows, bf(fox_w_out[0]),
                     vec(mix_norm_post[1]), ffn2(1), ple(1))
    return cube(h)
```
